```python
import jax, jax.numpy as jnp
from jax import lax
import numpy as np

D_MODEL = 1024
BATCH = 8
SEQ = 2048
DEPTH = 4
DEC_BATCH = 128
DEC_SEQ = 4
PAST_LEN = 8192
PAGE_SIZE = 128

SWA_HEADS = 8
SWA_KV_HEADS = 2
SWA_HEAD_DIM = 64
SWA_GROUP = SWA_HEADS // SWA_KV_HEADS
WINDOW = 128
ROPE_THETA = 10000.0
HG_HEADS = 4
HG_DK = 128
HG_DV = 128
HG_CHUNK = 64
LB_FLOOR = 1e-30
POOL_WINDOWS = (2, 4, 8, 16)
POOL_GROUPS = 4
POOL_GROUP_DIM = 128
POOL_WIDTH = POOL_GROUPS * POOL_GROUP_DIM
POOL_BUF = 15
BRANCH_WIDTH = 512
N_BRANCH = 3
D_FF = 4 * D_MODEL
EPS = 1e-6
NEG = -1e30

SWA_Q = SWA_HEADS * SWA_HEAD_DIM
SWA_KV = SWA_KV_HEADS * SWA_HEAD_DIM
HG_K = HG_HEADS * HG_DK
HG_V = HG_HEADS * HG_DV
GATE_W = N_BRANCH * D_MODEL
P1 = SWA_Q
P2 = P1 + SWA_KV
P3 = P2 + SWA_KV
P4 = P3 + HG_K
P5 = P4 + HG_K
P6 = P5 + HG_V
P7 = P6 + HG_V
P8 = P7 + POOL_WIDTH
D_IN = P8 + GATE_W
SPLIT_POINTS = (P1, P2, P3, P4, P5, P6, P7, P8)

kernel_name = "hybrid_hgrn2_swa_pool_gated_decode_step"

F32 = jnp.float32


def rmsnorm(x, w):
    xf = x.astype(F32)
    y = xf * lax.rsqrt(jnp.mean(xf * xf, axis=-1, keepdims=True) + EPS) * w.astype(F32)
    return y.astype(x.dtype)


def rope(x, pos):
    hd = x.shape[-1]
    inv = jnp.power(ROPE_THETA, -jnp.arange(0, hd, 2, dtype=F32) / hd)
    ang = pos[:, None] * inv[None, :]
    cos = jnp.cos(ang)[:, None, :]
    sin = jnp.sin(ang)[:, None, :]
    xf = x.astype(F32)
    x1, x2 = xf[..., : hd // 2], xf[..., hd // 2:]
    return jnp.concatenate([x1 * cos - x2 * sin, x2 * cos + x1 * sin], axis=-1).astype(x.dtype)


def sink_softmax(s, mask, sink):
    sink = sink.astype(F32)
    sm = jnp.where(mask, s, NEG)
    m = jnp.maximum(jnp.max(sm, axis=-1, keepdims=True), sink)
    e = jnp.where(mask, jnp.exp(sm - m), 0.0)
    return e / (jnp.sum(e, axis=-1, keepdims=True) + jnp.exp(sink - m))


def swa_prompt(q, k, v, sinks):
    B, T = q.shape[:2]
    nb = T // WINDOW
    qb = q.astype(F32).reshape(B, nb, WINDOW, SWA_KV_HEADS, SWA_GROUP, SWA_HEAD_DIM)

    def with_prev(a):
        a = a.astype(F32).reshape(B, nb, WINDOW, SWA_KV_HEADS, SWA_HEAD_DIM)
        prev = jnp.pad(a, ((0, 0), (1, 0), (0, 0), (0, 0), (0, 0)))[:, :-1]
        return jnp.concatenate([prev, a], axis=2)

    kk = with_prev(k)
    vv = with_prev(v)
    s = jnp.einsum('bnqkgd,bnskd->bnkgqs', qb, kk) * (SWA_HEAD_DIM ** -0.5)
    i = jnp.arange(WINDOW)[:, None]
    j = jnp.arange(2 * WINDOW)[None, :]
    band = (j > i) & (j <= i + WINDOW)
    blk = jnp.arange(nb)[:, None, None]
    mask = band[None] & ((blk > 0) | (j[None] >= WINDOW))
    p = sink_softmax(s, mask[None, :, None, None], sinks.reshape(SWA_KV_HEADS, SWA_GROUP)[:, :, None, None])
    o = jnp.einsum('bnkgqs,bnskd->bnqkgd', p, vv)
    return o.reshape(B, T, SWA_Q).astype(q.dtype)


def swa_sample(q, k, v, k_buf, v_buf, sinks):
    B, T = q.shape[:2]
    WB = k_buf.shape[1]
    kk = jnp.concatenate([k_buf.astype(k.dtype), k], axis=1)
    vv = jnp.concatenate([v_buf.astype(v.dtype), v], axis=1)
    qpos = PAST_LEN + jnp.arange(T)[:, None]
    kpos = PAST_LEN - WB + jnp.arange(WB + T)[None, :]
    mask = (kpos <= qpos) & (qpos - kpos < WINDOW)
    qg = q.astype(F32).reshape(B, T, SWA_KV_HEADS, SWA_GROUP, SWA_HEAD_DIM)
    s = jnp.einsum('btkgd,bskd->bkgts', qg, kk.astype(F32)) * (SWA_HEAD_DIM ** -0.5)
    p = sink_softmax(s, mask, sinks.reshape(SWA_KV_HEADS, SWA_GROUP)[:, :, None, None])
    o = jnp.einsum('bkgts,bskd->btkgd', p, vv.astype(F32))
    return o.reshape(B, T, SWA_Q).astype(q.dtype), kk[:, -WB:], vv[:, -WB:]


def hgrn2(q, logf, kg, inp, S0):
    B, T = q.shape[:2]
    C = min(HG_CHUNK, T)
    pad = (-T) % C
    padw = ((0, 0), (0, pad), (0, 0), (0, 0))
    n = (T + pad) // C

    def chunks(a):
        a = jnp.pad(a, padw)
        return a.reshape(B, n, C, a.shape[2], a.shape[3]).swapaxes(0, 1)

    causal = jnp.tril(jnp.ones((C, C), dtype=bool))[None, :, :, None, None]

    def step(S, xs):
        qc, gc, kc, ic = xs
        G = jnp.cumsum(gc, axis=1)
        o_inter = jnp.einsum('bchd,bhde->bche', qc * jnp.exp(G), S)
        diff = G[:, :, None] - G[:, None, :]
        dec = jnp.where(causal, jnp.exp(jnp.where(causal, diff, 0.0)), 0.0)
        A = jnp.einsum('bthd,btshd,bshd->bhts', qc, dec, kc)
        o_intra = jnp.einsum('bhts,bshe->bthe', A, ic)
        G_last = G[:, -1]
        S_new = jnp.exp(G_last)[..., None] * S + jnp.einsum(
            'bshd,bshe->bhde', kc * jnp.exp(G_last[:, None] - G), ic)
        return S_new, o_inter + o_intra

    S, o = lax.scan(step, S0, (chunks(q), chunks(logf), chunks(kg), chunks(inp)))
    o = o.swapaxes(0, 1).reshape(B, n * C, HG_HEADS, HG_DV)[:, :T]
    return o, S


def pool_mix(u_ext, T, w_pool, scale):
    B, L, _ = u_ext.shape
    uf = u_ext.astype(F32)
    cs = jnp.concatenate([jnp.zeros((B, 1, POOL_WIDTH), F32), jnp.cumsum(uf, axis=1)], axis=1)
    e = jnp.arange(L - T, L)
    outs = []
    for g, w in enumerate(POOL_WINDOWS):
        sl = slice(g * POOL_GROUP_DIM, (g + 1) * POOL_GROUP_DIM)
        hi = cs[:, e + 1, sl]
        lo = cs[:, jnp.maximum(e + 1 - w, 0), sl]
        cnt = jnp.minimum(w, e + 1).astype(F32)[None, :, None]
        outs.append((hi - lo) / cnt - uf[:, e, sl])
    d = jnp.concatenate(outs, axis=-1).reshape(B, T, POOL_GROUPS, POOL_GROUP_DIM)
    y = jnp.einsum('btgc,gcd->btgd', d, w_pool.astype(F32)).reshape(B, T, POOL_WIDTH)
    return (y * scale.astype(F32)).astype(u_ext.dtype)


def decoder_layer(x, pos, lw, st):
    (norm_mix, w_in, q_norm, k_norm, sinks, lb, onorm, pool_w, pool_scale,
     w_branch, w_o, norm_ffn, w_up, w_down) = lw
    B, T, _ = x.shape
    xn = rmsnorm(x, norm_mix)
    z = xn @ w_in
    zq, zk, zv, hq, hf, hi, hg, u, zg = jnp.split(z, SPLIT_POINTS, axis=-1)

    q = rope(rmsnorm(zq.reshape(B, T, SWA_HEADS, SWA_HEAD_DIM), q_norm), pos)
    k = rope(rmsnorm(zk.reshape(B, T, SWA_KV_HEADS, SWA_HEAD_DIM), k_norm), pos)
    v = zv.reshape(B, T, SWA_KV_HEADS, SWA_HEAD_DIM)
    if st is None:
        WB = min(WINDOW, T)
        o_b = swa_prompt(q, k, v, sinks)
        k_new, v_new = k[:, -WB:], v[:, -WB:]
        S0 = jnp.zeros((B, HG_HEADS, HG_DK, HG_DV), F32)
        s_dtype = x.dtype
        u_ext = u
    else:
        S_in, k_buf, v_buf, p_buf = st
        o_b, k_new, v_new = swa_sample(q, k, v, k_buf, v_buf, sinks)
        S0 = S_in.astype(F32)
        s_dtype = S_in.dtype
        u_ext = jnp.concatenate([p_buf.astype(u.dtype), u], axis=1)

    zf = hf.astype(F32).reshape(B, T, HG_HEADS, HG_DK)
    lbh = lb.reshape(HG_HEADS, HG_DK)
    logf = jnp.logaddexp(jnp.log(jnp.maximum(lbh, LB_FLOOR)), jnp.log1p(-lbh) + jax.nn.log_sigmoid(zf))
    kg = (1.0 - lbh) * jax.nn.sigmoid(-zf)
    qh = jax.nn.silu(hq.astype(F32)).reshape(B, T, HG_HEADS, HG_DK)
    ih = hi.astype(F32).reshape(B, T, HG_HEADS, HG_DV)
    o, S_new = hgrn2(qh, logf, kg, ih, S0)
    o = rmsnorm(o, onorm) * jax.nn.silu(hg.astype(F32).reshape(B, T, HG_HEADS, HG_DV))
    o_a = o.reshape(B, T, HG_V).astype(x.dtype)

    o_c = pool_mix(u_ext, T, pool_w, pool_scale)
    p_new = u_ext[:, -POOL_BUF:]

    br = jnp.stack([o_a, o_b, o_c], axis=-2)
    proj = jnp.einsum('btnc,ncd->btnd', br, w_branch)
    gates = jax.nn.sigmoid(zg.reshape(B, T, N_BRANCH, D_MODEL))
    x = x + jnp.sum(gates * proj, axis=-2) @ w_o

    h = rmsnorm(x, norm_ffn) @ w_up
    x = x + jnp.square(jax.nn.relu(h)) @ w_down
    return x, (S_new.astype(s_dtype), k_new, v_new, p_new)


def setup_inputs(seed: int = 0) -> dict:
    key = jax.random.key(seed)
    ks = jax.random.split(key, 20)
    WB = min(WINDOW, PAST_LEN)

    def nrm(k, shape, s):
        return jax.random.normal(k, shape, F32) * s

    return {
        "x_prompt": nrm(ks[0], (BATCH, SEQ, D_MODEL), 1.0),
        "x_sample": nrm(ks[1], (DEC_BATCH, DEC_SEQ, D_MODEL), 1.0),
        "state_hgrn": nrm(ks[2], (DEPTH, DEC_BATCH, HG_HEADS, HG_DK, HG_DV), 0.5),
        "cache_swa_k": nrm(ks[3], (DEPTH, DEC_BATCH, WB, SWA_KV_HEADS, SWA_HEAD_DIM), 1.0),
        "cache_swa_v": nrm(ks[4], (DEPTH, DEC_BATCH, WB, SWA_KV_HEADS, SWA_HEAD_DIM), 1.0),
        "state_pool": nrm(ks[5], (DEPTH, DEC_BATCH, POOL_BUF, POOL_WIDTH), 1.0),
        "norm_mix": 1.0 + nrm(ks[6], (DEPTH, D_MODEL), 0.05),
        "w_in": nrm(ks[7], (DEPTH, D_MODEL, D_IN), D_MODEL ** -0.5),
        "q_norm": 1.0 + nrm(ks[8], (DEPTH, SWA_HEAD_DIM), 0.05),
        "k_norm": 1.0 + nrm(ks[9], (DEPTH, SWA_HEAD_DIM), 0.05),
        "attn_sinks": nrm(ks[10], (DEPTH, SWA_HEADS), 0.5),
        "hgrn_lb": nrm(ks[11], (DEPTH, HG_K), 0.1),
        "hgrn_onorm": 1.0 + nrm(ks[12], (DEPTH, HG_DV), 0.05),
        "pool_w": nrm(ks[13], (DEPTH, POOL_GROUPS, POOL_GROUP_DIM, POOL_GROUP_DIM), POOL_GROUP_DIM ** -0.5),
        "pool_scale": 1.0 + nrm(ks[14], (DEPTH, POOL_WIDTH), 0.1),
        "w_branch": nrm(ks[15], (DEPTH, N_BRANCH, BRANCH_WIDTH, D_MODEL), BRANCH_WIDTH ** -0.5),
        "w_o": nrm(ks[16], (DEPTH, D_MODEL, D_MODEL), D_MODEL ** -0.5),
        "norm_ffn": 1.0 + nrm(ks[17], (DEPTH, D_MODEL), 0.05),
        "w_up": nrm(ks[18], (DEPTH, D_MODEL, D_FF), D_MODEL ** -0.5),
        "w_down": nrm(ks[19], (DEPTH, D_FF, D_MODEL), D_FF ** -0.5),
    }


def reference(x_prompt, x_sample, state_hgrn, cache_swa_k, cache_swa_v, state_pool,
              norm_mix, w_in, q_norm, k_norm, attn_sinks, hgrn_lb, hgrn_onorm,
              pool_w, pool_scale, w_branch, w_o, norm_ffn, w_up, w_down):
    lbp = jax.nn.softmax(hgrn_lb.astype(F32), axis=0)
    lower = jnp.maximum(jnp.cumsum(lbp, axis=0) - lbp[0], 0.0)
    pos_p = jnp.arange(x_prompt.shape[1], dtype=F32)
    pos_s = jnp.arange(x_sample.shape[1], dtype=F32) + PAST_LEN
    hp, hs = x_prompt, x_sample
    sp = ([], [], [], [])
    ss = ([], [], [], [])
    for l in range(DEPTH):
        lw = (norm_mix[l], w_in[l], q_norm[l], k_norm[l], attn_sinks[l], lower[l], hgrn_onorm[l],
              pool_w[l], pool_scale[l], w_branch[l], w_o[l], norm_ffn[l], w_up[l], w_down[l])
        hp, new_p = decoder_layer(hp, pos_p, lw, None)
        hs, new_s = decoder_layer(hs, pos_s, lw, (state_hgrn[l], cache_swa_k[l], cache_swa_v[l], state_pool[l]))
        for lst, a in zip(sp, new_p):
            lst.append(a)
        for lst, a in zip(ss, new_s):
            lst.append(a)
    return (hp, hs,
            jnp.stack(sp[0]), jnp.stack(sp[1]), jnp.stack(sp[2]), jnp.stack(sp[3]),
            jnp.stack(ss[0]), jnp.stack(ss[1]), jnp.stack(ss[2]), jnp.stack(ss[3]))
```

```python
import functools

import jax
import jax.numpy as jnp
from jax import lax
from jax.experimental import pallas as pl
from jax.experimental.pallas import tpu as pltpu

F32 = jnp.float32
BF16 = jnp.bfloat16

D_MODEL = 1024
PAST_LEN = 8192
SWA_HEADS = 8
SWA_KV_HEADS = 2
SWA_HEAD_DIM = 64
SWA_GROUP = SWA_HEADS // SWA_KV_HEADS
WINDOW = 128
ROPE_THETA = 10000.0
HG_HEADS = 4
HG_DK = 128
HG_DV = 128
LB_FLOOR = 1e-30
POOL_WINDOWS = (2, 4, 8, 16)
POOL_GROUP_DIM = 128
POOL_WIDTH = 512
POOL_BUF = 15
BRANCH_WIDTH = 512
N_BRANCH = 3
D_FF = 4 * D_MODEL
EPS = 1e-6
NEG = -1e30

SWA_Q = SWA_HEADS * SWA_HEAD_DIM
SWA_KV = SWA_KV_HEADS * SWA_HEAD_DIM
HG_K = HG_HEADS * HG_DK
QKV_W = SWA_Q + 2 * SWA_KV
P3 = QKV_W
P4 = P3 + HG_K
P5 = P4 + HG_K
P6 = P5 + HG_K
P7 = P6 + HG_K
P8 = P7 + POOL_WIDTH
D_IN = P8 + N_BRANCH * D_MODEL

VMEM_LIMIT_BYTES = 56 * 1024 * 1024
HG_CHUNK = 128
HG_DIAG = 8
POOL_TILE = 256
POOL_CARRY = 16
SAMPLE_BB = 8


def _cparams(*sem):
    return pltpu.CompilerParams(dimension_semantics=sem, vmem_limit_bytes=VMEM_LIMIT_BYTES)


def _rms_rows(x, w):
    ms = jnp.mean(x * x, axis=-1, keepdims=True)
    return x * lax.rsqrt(ms + EPS) * w


def _sigmoid_pair(z):
    e = jnp.exp(-jnp.abs(z))
    r = 1.0 / (1.0 + e)
    er = e * r
    pos = z >= 0
    return jnp.where(pos, r, er), jnp.where(pos, er, r)


def _silu(z):
    return z * _sigmoid_pair(z)[0]


def _split3(x):
    hi = x.astype(BF16)
    r1 = x - hi.astype(F32)
    mid = r1.astype(BF16)
    lo = (r1 - mid.astype(F32)).astype(BF16)
    return hi, mid, lo


def _dot(a, b):
    return jnp.dot(a, b, preferred_element_type=F32)


def _dot_nt(a, b):
    return lax.dot_general(a, b, (((1,), (1,)), ((), ())), preferred_element_type=F32)


def _dot_tn(a, b):
    return lax.dot_general(a, b, (((0,), (0,)), ((), ())), preferred_element_type=F32)


def _lower_kernel(lb_ref, o_ref):
    x = lb_ref[...]
    depth = x.shape[0]
    rows = [x[i:i + 1] for i in range(depth)]
    m = rows[0]
    for r in rows[1:]:
        m = jnp.maximum(m, r)
    es = [jnp.exp(r - m) for r in rows]
    tot = es[0]
    for e in es[1:]:
        tot = tot + e
    ps = [e / tot for e in es]
    out = []
    c = None
    for p in ps:
        c = p if c is None else c + p
        out.append(jnp.maximum(c - ps[0], 0.0))
    o_ref[...] = jnp.concatenate(out, axis=0)


def _lower_bounds(hgrn_lb):
    return pl.pallas_call(
        _lower_kernel,
        out_shape=jax.ShapeDtypeStruct(hgrn_lb.shape, F32),
        name="lower_bounds",
    )(hgrn_lb)


def _proj_in_kernel(x_ref, nw_ref, w_ref, qkv_ref, hq_ref, hf_ref, hi_ref, hg_ref, u_ref, zg_ref):
    xn = _rms_rows(x_ref[...], nw_ref[...]).astype(BF16)

    def mm(a, b):
        return _dot(xn, w_ref[:, a:b])

    qkv_ref[...] = mm(0, P3)
    hq_ref[...] = mm(P3, P4)
    hf_ref[...] = mm(P4, P5)
    hi_ref[...] = mm(P5, P6)
    hg_ref[...] = mm(P6, P7)
    u_ref[...] = mm(P7, P8)
    for n in range(N_BRANCH):
        zg_ref[:, n * D_MODEL:(n + 1) * D_MODEL] = mm(P8 + n * D_MODEL, P8 + (n + 1) * D_MODEL)


def _proj_in(x, norm_w, w_in_bf, layer, tm):
    m = x.shape[0]
    widths = (QKV_W, HG_K, HG_K, HG_K, HG_K, POOL_WIDTH, N_BRANCH * D_MODEL)
    return pl.pallas_call(
        _proj_in_kernel,
        grid=(m // tm,),
        in_specs=[
            pl.BlockSpec((tm, D_MODEL), lambda i: (i, 0)),
            pl.BlockSpec((None, 1, D_MODEL), lambda i: (layer, 0, 0)),
            pl.BlockSpec((None, D_MODEL, D_IN), lambda i: (layer, 0, 0), pipeline_mode=pl.Buffered(1)),
        ],
        out_specs=[pl.BlockSpec((tm, w), lambda i: (i, 0)) for w in widths],
        out_shape=[jax.ShapeDtypeStruct((m, w), F32) for w in widths],
        compiler_params=_cparams("arbitrary"),
        name="proj_in",
    )(x, norm_w, w_in_bf)


def _head_norm_rope(x, w_row, cos, sin, bd):
    width = x.shape[1]
    sq = x * x
    hi = sq.astype(BF16)
    lo = (sq - hi.astype(F32)).astype(BF16)
    parts = []
    for c in range(0, width, 256):
        wc = min(256, width - c)
        b = bd[:wc, :wc]
        parts.append(_dot(hi[:, c:c + wc], b) + _dot(lo[:, c:c + wc], b))
    ms = (parts[0] if len(parts) == 1 else jnp.concatenate(parts, axis=1)) * (1.0 / SWA_HEAD_DIM)
    y = x * lax.rsqrt(ms + EPS) * w_row
    reps = width // 128
    cos_t = cos if reps == 1 else jnp.concatenate([cos] * reps, axis=1)
    sin_t = sin if reps == 1 else jnp.concatenate([sin] * reps, axis=1)
    lane = lax.broadcasted_iota(jnp.int32, y.shape, 1)
    first_half = (lane & (SWA_HEAD_DIM - 1)) < (SWA_HEAD_DIM // 2)
    half = SWA_HEAD_DIM // 2
    rot = jnp.where(first_half, pltpu.roll(y, width - half, 1), pltpu.roll(y, half, 1))
    return y * cos_t + rot * sin_t


def _swa_prompt_kernel(sinks_ref, qkv_ref, cos_ref, sin_ref, qn_ref, kn_ref, bd_ref,
                       o_ref, kc_ref, kprev, vprev, *, layer):
    n = pl.program_id(1)

    @pl.when(n == 0)
    def _():
        kprev[...] = jnp.zeros_like(kprev)
        vprev[...] = jnp.zeros_like(vprev)

    z = qkv_ref[...]
    cos = cos_ref[...]
    sin = sin_ref[...]
    bd = bd_ref[...]
    q = _head_norm_rope(z[:, :SWA_Q], qn_ref[...], cos, sin, bd)
    k = _head_norm_rope(z[:, SWA_Q:SWA_Q + SWA_KV], kn_ref[...], cos, sin, bd)
    v = z[:, SWA_Q + SWA_KV:]
    kp = kprev[...]
    vp = vprev[...]

    rows = SWA_GROUP * WINDOW
    i = lax.broadcasted_iota(jnp.int32, (rows, 2 * WINDOW), 0) & (WINDOW - 1)
    j = lax.broadcasted_iota(jnp.int32, (rows, 2 * WINDOW), 1)
    mask = (j > i) & (j <= i + WINDOW) & ((n > 0) | (j >= WINDOW))
    scale = SWA_HEAD_DIM ** -0.5
    outs = []
    for kvh in range(SWA_KV_HEADS):
        ks = slice(kvh * SWA_HEAD_DIM, (kvh + 1) * SWA_HEAD_DIM)
        kext = jnp.concatenate([kp[:, ks], k[:, ks]], axis=0).astype(BF16)
        vext = jnp.concatenate([vp[:, ks], v[:, ks]], axis=0).astype(BF16)
        heads = [kvh * SWA_GROUP + g for g in range(SWA_GROUP)]
        qg = jnp.concatenate([q[:, h * SWA_HEAD_DIM:(h + 1) * SWA_HEAD_DIM] for h in heads], axis=0)
        s = _dot_nt(qg.astype(BF16), kext) * scale
        sink = jnp.concatenate([jnp.full((WINDOW, 1), sinks_ref[layer, h], F32) for h in heads], axis=0)
        sm = jnp.where(mask, s, NEG)
        mx = jnp.maximum(jnp.max(sm, axis=-1, keepdims=True), sink)
        e = jnp.where(mask, jnp.exp(sm - mx), 0.0)
        p = e / (jnp.sum(e, axis=-1, keepdims=True) + jnp.exp(sink - mx))
        og = _dot(p.astype(BF16), vext)
        outs.extend(og[g * WINDOW:(g + 1) * WINDOW] for g in range(SWA_GROUP))
    o_ref[...] = jnp.concatenate(outs, axis=1)
    kc_ref[...] = k
    kprev[...] = k
    vprev[...] = v


def _swa_prompt(qkv, cos, sin, qn, kn, bd, sinks, layer, batch, seq):
    nb = seq // WINDOW
    row = lambda b, n: (b * nb + n, 0)
    const = lambda b, n: (0, 0)
    return pl.pallas_call(
        functools.partial(_swa_prompt_kernel, layer=layer),
        grid=(batch, nb),
        in_specs=[
            pl.BlockSpec(memory_space=pltpu.SMEM),
            pl.BlockSpec((WINDOW, QKV_W), row),
            pl.BlockSpec((WINDOW, 128), lambda b, n: (n, 0)),
            pl.BlockSpec((WINDOW, 128), lambda b, n: (n, 0)),
            pl.BlockSpec((1, SWA_Q), const),
            pl.BlockSpec((1, SWA_KV), const),
            pl.BlockSpec((256, 256), const),
        ],
        out_specs=[
            pl.BlockSpec((WINDOW, SWA_Q), row),
            pl.BlockSpec((None, WINDOW, SWA_KV), lambda b, n: (b, 0, 0)),
        ],
        out_shape=[
            jax.ShapeDtypeStruct((batch * seq, SWA_Q), F32),
            jax.ShapeDtypeStruct((batch, WINDOW, SWA_KV), F32),
        ],
        scratch_shapes=[pltpu.VMEM((WINDOW, SWA_KV), F32), pltpu.VMEM((WINDOW, SWA_KV), F32)],
        compiler_params=_cparams("arbitrary", "arbitrary"),
        name="swa_prompt",
    )(sinks, qkv, cos, sin, qn, kn, bd)


def _swa_sample_kernel(sinks_ref, qkv_ref, cos_ref, sin_ref, qn_ref, kn_ref, bd_ref, kb_ref, vb_ref,
                       o_ref, kn_out_ref, *, layer, t_new):
    z = qkv_ref[...]
    cos = cos_ref[...]
    sin = sin_ref[...]
    bd = bd_ref[...]
    q = _head_norm_rope(z[:, :SWA_Q], qn_ref[...], cos, sin, bd)
    k = _head_norm_rope(z[:, SWA_Q:SWA_Q + SWA_KV], kn_ref[...], cos, sin, bd)
    v = z[:, SWA_Q + SWA_KV:]
    kn_out_ref[...] = k

    wb = kb_ref.shape[1]
    rows = SWA_GROUP * t_new
    t_idx = lax.broadcasted_iota(jnp.int32, (rows, 1), 0) % t_new
    j = lax.broadcasted_iota(jnp.int32, (rows, wb), 1)
    mask_c = (wb + t_idx - j) < WINDOW
    scale = SWA_HEAD_DIM ** -0.5
    per_b = []
    for b in range(SAMPLE_BB):
        r0 = b * t_new
        kb = kb_ref[b]
        vb = vb_ref[b]
        outs = []
        for kvh in range(SWA_KV_HEADS):
            ks = slice(kvh * SWA_HEAD_DIM, (kvh + 1) * SWA_HEAD_DIM)
            heads = [kvh * SWA_GROUP + g for g in range(SWA_GROUP)]
            qg = jnp.concatenate(
                [q[r0:r0 + t_new, h * SWA_HEAD_DIM:(h + 1) * SWA_HEAD_DIM] for h in heads], axis=0)
            qg_bf = qg.astype(BF16)
            qg_r = qg_bf.astype(F32)
            s_c = _dot_nt(qg_bf, kb[:, ks].astype(BF16)) * scale
            sink = jnp.concatenate([jnp.full((t_new, 1), sinks_ref[layer, h], F32) for h in heads], axis=0)
            sm_c = jnp.where(mask_c, s_c, NEG)
            mx = jnp.maximum(jnp.max(sm_c, axis=-1, keepdims=True), sink)
            kn = k[r0:r0 + t_new, ks].astype(BF16).astype(F32)
            vn = v[r0:r0 + t_new, ks].astype(BF16).astype(F32)
            s_n = []
            for i in range(t_new):
                si = jnp.sum(qg_r * kn[i:i + 1], axis=-1, keepdims=True) * scale
                si = jnp.where(t_idx >= i, si, NEG)
                s_n.append(si)
                mx = jnp.maximum(mx, si)
            e_c = jnp.where(mask_c, jnp.exp(sm_c - mx), 0.0)
            den = jnp.sum(e_c, axis=-1, keepdims=True) + jnp.exp(sink - mx)
            e_n = []
            for i in range(t_new):
                ei = jnp.where(t_idx >= i, jnp.exp(s_n[i] - mx), 0.0)
                e_n.append(ei)
                den = den + ei
            og = _dot((e_c / den).astype(BF16), vb[:, ks].astype(BF16))
            for i in range(t_new):
                og = og + (e_n[i] / den).astype(BF16).astype(F32) * vn[i:i + 1]
            outs.extend(og[g * t_new:(g + 1) * t_new] for g in range(SWA_GROUP))
        per_b.append(jnp.concatenate(outs, axis=1))
    o_ref[...] = jnp.concatenate(per_b, axis=0)


def _swa_sample(qkv, cos, sin, qn, kn, bd, sinks, kbuf, vbuf, layer, batch, t_new):
    rb = SAMPLE_BB * t_new
    wb = kbuf.shape[2]
    row = lambda i: (i, 0)
    const = lambda i: (0, 0)
    return pl.pallas_call(
        functools.partial(_swa_sample_kernel, layer=layer, t_new=t_new),
        grid=(batch // SAMPLE_BB,),
        in_specs=[
            pl.BlockSpec(memory_space=pltpu.SMEM),
            pl.BlockSpec((rb, QKV_W), row),
            pl.BlockSpec((rb, 128), const),
            pl.BlockSpec((rb, 128), const),
            pl.BlockSpec((1, SWA_Q), const),
            pl.BlockSpec((1, SWA_KV), const),
            pl.BlockSpec((256, 256), const),
            pl.BlockSpec((None, SAMPLE_BB, wb, SWA_KV), lambda i: (layer, i, 0, 0)),
            pl.BlockSpec((None, SAMPLE_BB, wb, SWA_KV), lambda i: (layer, i, 0, 0)),
        ],
        out_specs=[pl.BlockSpec((rb, SWA_Q), row), pl.BlockSpec((rb, SWA_KV), row)],
        out_shape=[
            jax.ShapeDtypeStruct((batch * t_new, SWA_Q), F32),
            jax.ShapeDtypeStruct((batch * t_new, SWA_KV), F32),
        ],
        compiler_params=_cparams("arbitrary"),
        name="swa_sample",
    )(sinks, qkv, cos, sin, qn, kn, bd, kbuf, vbuf)


def _hgrn_gates(hq, hf, lb):
    sig_p, sig_n = _sigmoid_pair(hf)
    om = 1.0 - lb
    logf = jnp.log(jnp.maximum(lb, LB_FLOOR) + om * sig_p)
    return _silu(hq), logf, om * sig_n


def _hgrn_out(o, onorm, hg):
    outs = []
    for h in range(HG_HEADS):
        hs = slice(h * HG_DV, (h + 1) * HG_DV)
        outs.append(_rms_rows(o[h], onorm) * _silu(hg[:, hs]))
    return jnp.concatenate(outs, axis=1)


def _col_bcast(row):
    return jnp.transpose(jnp.broadcast_to(row, (row.shape[1], row.shape[1])))


def _hgrn_prompt_kernel(hq_ref, hf_ref, hi_ref, hg_ref, lb_ref, on_ref, tri_ref,
                        o_ref, s_out_ref, s_scr):
    n = pl.program_id(1)

    @pl.when(n == 0)
    def _():
        s_scr[...] = jnp.zeros_like(s_scr)

    c = HG_CHUNK
    q, g, kg = _hgrn_gates(hq_ref[...], hf_ref[...], lb_ref[...])
    v = hi_ref[...]
    tri = tri_ref[...]
    g_hi, g_mid, g_lo = _split3(g)
    G = _dot(tri, g_hi) + _dot(tri, g_mid) + _dot(tri, g_lo)

    t_col = lax.broadcasted_iota(jnp.int32, (c, 1), 0)
    ti = lax.broadcasted_iota(jnp.int32, (c, c), 0)
    si = lax.broadcasted_iota(jnp.int32, (c, c), 1)

    levels = []
    half = HG_DIAG
    while half < c:
        blk = 2 * half
        gb = jnp.concatenate(
            [jnp.broadcast_to(G[p * blk + half - 1:p * blk + half, :], (blk, G.shape[1]))
             for p in range(c // blk)], axis=0)
        second = (t_col & half) != 0
        e = jnp.exp(jnp.minimum(jnp.where(second, G - gb, gb - G), 0.0))
        qt = jnp.where(second, q * e, 0.0).astype(BF16)
        kt = jnp.where(second, 0.0, kg * e).astype(BF16)
        shift = blk.bit_length() - 1
        levels.append((qt, kt, (ti >> shift) == (si >> shift)))
        half = blk

    g_last = G[c - 1:c, :]
    q_in = (q * jnp.exp(G)).astype(BF16)
    k_out = (kg * jnp.exp(jnp.minimum(g_last - G, 0.0))).astype(BF16)
    dec_last = jnp.exp(g_last)
    v_bf = v.astype(BF16)

    nblk = c // HG_DIAG
    row_in_blk = lax.broadcasted_iota(jnp.int32, (nblk, HG_DIAG, 1), 1)
    o_heads = []
    for h in range(HG_HEADS):
        hs = slice(h * HG_DK, (h + 1) * HG_DK)
        s_h = s_scr[h]
        a_off = jnp.zeros((c, c), F32)
        for qt, kt, same in levels:
            a_off = a_off + jnp.where(same, _dot_nt(qt[:, hs], kt[:, hs]), 0.0)
        o_h = _dot(q_in[:, hs], s_h.astype(BF16)) + _dot(a_off.astype(BF16), v_bf[:, hs])
        q3 = q[:, hs].reshape(nblk, HG_DIAG, HG_DK)
        k3 = kg[:, hs].reshape(nblk, HG_DIAG, HG_DK)
        g3 = G[:, hs].reshape(nblk, HG_DIAG, HG_DK)
        v3 = v[:, hs].reshape(nblk, HG_DIAG, HG_DV)
        od = jnp.zeros((nblk, HG_DIAG, HG_DV), F32)
        for s in range(HG_DIAG):
            e = jnp.exp(jnp.minimum(g3 - g3[:, s:s + 1, :], 0.0))
            a = jnp.sum(q3 * (k3[:, s:s + 1, :] * e), axis=-1, keepdims=True)
            od = od + jnp.where(row_in_blk >= s, a, 0.0) * v3[:, s:s + 1, :]
        o_heads.append(o_h + od.reshape(c, HG_DV))
        s_new = _col_bcast(dec_last[:, hs]) * s_h + _dot_tn(k_out[:, hs], v_bf[:, hs])
        s_scr[h] = s_new
        s_out_ref[h] = s_new
    o_ref[...] = _hgrn_out(o_heads, on_ref[...], hg_ref[...])


def _hgrn_prompt(hq, hf, hi, hg, lower, onorm, tri, layer, batch, seq):
    nc = seq // HG_CHUNK
    row = lambda b, n: (b * nc + n, 0)
    blk = pl.BlockSpec((HG_CHUNK, HG_K), row)
    return pl.pallas_call(
        _hgrn_prompt_kernel,
        grid=(batch, nc),
        in_specs=[
            blk, blk, blk, blk,
            pl.BlockSpec((None, 1, HG_K), lambda b, n: (layer, 0, 0)),
            pl.BlockSpec((None, 1, HG_DV), lambda b, n: (layer, 0, 0)),
            pl.BlockSpec((HG_CHUNK, HG_CHUNK), lambda b, n: (0, 0)),
        ],
        out_specs=[blk, pl.BlockSpec((None, HG_HEADS, HG_DK, HG_DV), lambda b, n: (b, 0, 0, 0))],
        out_shape=[
            jax.ShapeDtypeStruct((batch * seq, HG_K), F32),
            jax.ShapeDtypeStruct((batch, HG_HEADS, HG_DK, HG_DV), F32),
        ],
        scratch_shapes=[pltpu.VMEM((HG_HEADS, HG_DK, HG_DV), F32)],
        compiler_params=_cparams("arbitrary", "arbitrary"),
        name="hgrn_prompt",
    )(hq, hf, hi, hg, lower, onorm, tri)


def _hgrn_sample_kernel(hq_ref, hf_ref, hi_ref, hg_ref, lb_ref, on_ref, s0_ref,
                        o_ref, s_out_ref, *, t_new):
    q, g, kg = _hgrn_gates(hq_ref[...], hf_ref[...], lb_ref[...])
    v = hi_ref[...]
    rows = q.shape[0]
    t_idx = lax.broadcasted_iota(jnp.int32, (rows, 1), 0) % t_new

    def down(x, d):
        return pltpu.roll(x, d, 0)

    def up(x, d):
        return pltpu.roll(x, rows - d, 0)

    G = g
    for d in range(1, t_new):
        G = G + jnp.where(t_idx >= d, down(g, d), 0.0)
    rest = jnp.zeros_like(g)
    for d in range(1, t_new):
        rest = rest + jnp.where(t_idx + d < t_new, up(g, d), 0.0)
    q_in = (q * jnp.exp(G)).astype(BF16)
    k_out = kg * jnp.exp(rest)
    dec_all = jnp.exp(G + rest)

    o_intra = [jnp.zeros((rows, HG_DV), F32) for _ in range(HG_HEADS)]
    for d in range(t_new):
        kd = kg if d == 0 else down(kg, d)
        vd = v if d == 0 else down(v, d)
        gd = G if d == 0 else down(G, d)
        w = q * kd * jnp.exp(jnp.minimum(G - gd, 0.0))
        for h in range(HG_HEADS):
            hs = slice(h * HG_DK, (h + 1) * HG_DK)
            a = jnp.sum(w[:, hs], axis=-1, keepdims=True)
            o_intra[h] = o_intra[h] + jnp.where(t_idx >= d, a, 0.0) * vd[:, hs]

    pad = jnp.zeros((HG_DK - 2 * rows, HG_DK), F32)
    o_heads = [[] for _ in range(HG_HEADS)]
    for h in range(HG_HEADS):
        hs = slice(h * HG_DK, (h + 1) * HG_DK)
        cols = jnp.transpose(jnp.concatenate([k_out[:, hs], dec_all[:, hs], pad], axis=0))
        for b in range(SAMPLE_BB):
            r0 = b * t_new
            s_h = s0_ref[b, h]
            o_heads[h].append(_dot(q_in[r0:r0 + t_new, hs], s_h.astype(BF16)))
            s_new = cols[:, rows + r0:rows + r0 + 1] * s_h
            for s in range(t_new):
                s_new = s_new + cols[:, r0 + s:r0 + s + 1] * v[r0 + s:r0 + s + 1, hs]
            s_out_ref[b, h] = s_new
    o = [jnp.concatenate(o_heads[h], axis=0) + o_intra[h] for h in range(HG_HEADS)]
    o_ref[...] = _hgrn_out(o, on_ref[...], hg_ref[...])


def _hgrn_sample(hq, hf, hi, hg, lower, onorm, state, layer, batch, t_new):
    rb = SAMPLE_BB * t_new
    blk = pl.BlockSpec((rb, HG_K), lambda i: (i, 0))
    return pl.pallas_call(
        functools.partial(_hgrn_sample_kernel, t_new=t_new),
        grid=(batch // SAMPLE_BB,),
        in_specs=[
            blk, blk, blk, blk,
            pl.BlockSpec((None, 1, HG_K), lambda i: (layer, 0, 0)),
            pl.BlockSpec((None, 1, HG_DV), lambda i: (layer, 0, 0)),
            pl.BlockSpec((None, SAMPLE_BB, HG_HEADS, HG_DK, HG_DV), lambda i: (layer, i, 0, 0, 0)),
        ],
        out_specs=[blk, pl.BlockSpec((SAMPLE_BB, HG_HEADS, HG_DK, HG_DV), lambda i: (i, 0, 0, 0))],
        out_shape=[
            jax.ShapeDtypeStruct((batch * t_new, HG_K), F32),
            jax.ShapeDtypeStruct((batch, HG_HEADS, HG_DK, HG_DV), state.dtype),
        ],
        compiler_params=_cparams("arbitrary"),
        name="hgrn_sample",
    )(hq, hf, hi, hg, lower, onorm, state)


def _pool_project(d_groups, pw_ref, ps_ref):
    outs = []
    for gi, d in enumerate(d_groups):
        cs = slice(gi * POOL_GROUP_DIM, (gi + 1) * POOL_GROUP_DIM)
        outs.append(_dot(d.astype(BF16), pw_ref[gi]) * ps_ref[:, cs])
    return jnp.concatenate(outs, axis=1)


def _pool_prompt_kernel(u_ref, pw_ref, ps_ref, o_ref, carry):
    n = pl.program_id(1)

    @pl.when(n == 0)
    def _():
        carry[...] = jnp.zeros_like(carry)

    tt = u_ref.shape[0]
    u = u_ref[...]
    x = jnp.concatenate([carry[...], u], axis=0)
    sums = []
    cur = x
    shift = 1
    for gi in range(len(POOL_WINDOWS)):
        cur = cur + pltpu.roll(cur, shift, 0)
        sums.append(cur[POOL_CARRY:, :POOL_GROUP_DIM])
        cur = cur[:, POOL_GROUP_DIM:]
        shift *= 2
    pos = n * tt + lax.broadcasted_iota(jnp.int32, (tt, 1), 0)
    d_groups = []
    for gi, w in enumerate(POOL_WINDOWS):
        cs = slice(gi * POOL_GROUP_DIM, (gi + 1) * POOL_GROUP_DIM)
        cnt = jnp.minimum(w, pos + 1).astype(F32)
        d_groups.append(sums[gi] / cnt - u[:, cs])
    o_ref[...] = _pool_project(d_groups, pw_ref, ps_ref)
    carry[...] = u[tt - POOL_CARRY:, :]


def _pool_prompt(u, pool_w_bf, pool_scale, layer, batch, seq):
    nt = seq // POOL_TILE
    row = lambda b, n: (b * nt + n, 0)
    return pl.pallas_call(
        _pool_prompt_kernel,
        grid=(batch, nt),
        in_specs=[
            pl.BlockSpec((POOL_TILE, POOL_WIDTH), row),
            pl.BlockSpec((None, len(POOL_WINDOWS), POOL_GROUP_DIM, POOL_GROUP_DIM), lambda b, n: (layer, 0, 0, 0)),
            pl.BlockSpec((None, 1, POOL_WIDTH), lambda b, n: (layer, 0, 0)),
        ],
        out_specs=pl.BlockSpec((POOL_TILE, POOL_WIDTH), row),
        out_shape=jax.ShapeDtypeStruct((batch * seq, POOL_WIDTH), F32),
        scratch_shapes=[pltpu.VMEM((POOL_CARRY, POOL_WIDTH), F32)],
        compiler_params=_cparams("arbitrary", "arbitrary"),
        name="pool_prompt",
    )(u, pool_w_bf, pool_scale)


def _pool_sample_kernel(pb_ref, u_ref, pw_ref, ps_ref, o_ref, *, t_new):
    def row(i):
        return pb_ref[i] if i < POOL_BUF else u_ref[i - POOL_BUF]

    outs = []
    for t in range(t_new):
        e = POOL_BUF + t
        d_groups = []
        for gi, w in enumerate(POOL_WINDOWS):
            cs = slice(gi * POOL_GROUP_DIM, (gi + 1) * POOL_GROUP_DIM)
            lo = max(e + 1 - w, 0)
            acc = row(lo)[:, cs]
            for i in range(lo + 1, e + 1):
                acc = acc + row(i)[:, cs]
            d_groups.append(acc / float(e + 1 - lo) - u_ref[t][:, cs])
        outs.append(_pool_project(d_groups, pw_ref, ps_ref))
    for t in range(t_new):
        o_ref[t] = outs[t]


def _pool_sample(pbuf_t, u_t, pool_w_bf, pool_scale, layer, t_new):
    batch = u_t.shape[1]
    return pl.pallas_call(
        functools.partial(_pool_sample_kernel, t_new=t_new),
        grid=(1,),
        in_specs=[
            pl.BlockSpec((POOL_BUF, batch, POOL_WIDTH), lambda i: (0, 0, 0)),
            pl.BlockSpec((t_new, batch, POOL_WIDTH), lambda i: (0, 0, 0)),
            pl.BlockSpec((None, len(POOL_WINDOWS), POOL_GROUP_DIM, POOL_GROUP_DIM), lambda i: (layer, 0, 0, 0)),
            pl.BlockSpec((None, 1, POOL_WIDTH), lambda i: (layer, 0, 0)),
        ],
        out_specs=pl.BlockSpec((t_new, batch, POOL_WIDTH), lambda i: (0, 0, 0)),
        out_shape=jax.ShapeDtypeStruct((t_new, batch, POOL_WIDTH), F32),
        compiler_params=_cparams("arbitrary"),
        name="pool_sample",
    )(pbuf_t, u_t, pool_w_bf, pool_scale)


def _merge_kernel(x_ref, oa_ref, ob_ref, oc_ref, zg_ref, wb_ref, wo_ref, o_ref):
    m = None
    for n, br in enumerate((oa_ref, ob_ref, oc_ref)):
        proj = _dot(br[...].astype(BF16), wb_ref[n])
        gate = _sigmoid_pair(zg_ref[:, n * D_MODEL:(n + 1) * D_MODEL])[0]
        m = gate * proj if m is None else m + gate * proj
    o_ref[...] = x_ref[...] + _dot(m.astype(BF16), wo_ref[...])


def _merge(x, oa, ob, oc, zg, w_branch_bf, w_o_bf, layer, tm):
    m = x.shape[0]
    row = lambda i: (i, 0)
    return pl.pallas_call(
        _merge_kernel,
        grid=(m // tm,),
        in_specs=[
            pl.BlockSpec((tm, D_MODEL), row),
            pl.BlockSpec((tm, BRANCH_WIDTH), row),
            pl.BlockSpec((tm, BRANCH_WIDTH), row),
            pl.BlockSpec((tm, BRANCH_WIDTH), row),
            pl.BlockSpec((tm, N_BRANCH * D_MODEL), row),
            pl.BlockSpec((None, N_BRANCH, BRANCH_WIDTH, D_MODEL), lambda i: (layer, 0, 0, 0)),
            pl.BlockSpec((None, D_MODEL, D_MODEL), lambda i: (layer, 0, 0)),
        ],
        out_specs=pl.BlockSpec((tm, D_MODEL), row),
        out_shape=jax.ShapeDtypeStruct((m, D_MODEL), F32),
        compiler_params=_cparams("arbitrary"),
        name="merge",
    )(x, oa, ob, oc, zg, w_branch_bf, w_o_bf)


def _ffn_kernel(x_ref, nw_ref, wu_ref, wd_ref, o_ref):
    x = x_ref[...]
    xn = _rms_rows(x, nw_ref[...]).astype(BF16)
    acc = x
    step = D_MODEL
    for c in range(0, D_FF, step):
        h = _dot(xn, wu_ref[:, c:c + step])
        h = jnp.square(jnp.maximum(h, 0.0)).astype(BF16)
        acc = acc + _dot(h, wd_ref[c:c + step, :])
    o_ref[...] = acc


def _ffn(x, norm_w, w_up_bf, w_down_bf, layer, tm):
    m = x.shape[0]
    row = lambda i: (i, 0)
    return pl.pallas_call(
        _ffn_kernel,
        grid=(m // tm,),
        in_specs=[
            pl.BlockSpec((tm, D_MODEL), row),
            pl.BlockSpec((None, 1, D_MODEL), lambda i: (layer, 0, 0)),
            pl.BlockSpec((None, D_MODEL, D_FF), lambda i: (layer, 0, 0), pipeline_mode=pl.Buffered(1)),
            pl.BlockSpec((None, D_FF, D_MODEL), lambda i: (layer, 0, 0), pipeline_mode=pl.Buffered(1)),
        ],
        out_specs=pl.BlockSpec((tm, D_MODEL), row),
        out_shape=jax.ShapeDtypeStruct((m, D_MODEL), F32),
        compiler_params=_cparams("arbitrary"),
        name="ffn",
    )(x, norm_w, w_up_bf, w_down_bf)


def _rope_tables(pos):
    hd = SWA_HEAD_DIM
    inv = jnp.power(ROPE_THETA, -jnp.arange(0, hd, 2, dtype=F32) / hd)
    ang = pos[:, None] * inv[None, :]
    cos = jnp.cos(ang)
    sin = jnp.sin(ang)
    cos_t = jnp.concatenate([cos, cos, cos, cos], axis=1)
    sin_t = jnp.concatenate([-sin, sin, -sin, sin], axis=1)
    return cos_t, sin_t


def _row_tile(m, cap):
    t = min(m, cap)
    while m % t:
        t //= 2
    return t


def kernel(x_prompt, x_sample, state_hgrn, cache_swa_k, cache_swa_v, state_pool, norm_mix, w_in, q_norm, k_norm,
           attn_sinks, hgrn_lb, hgrn_onorm, pool_w, pool_scale, w_branch, w_o, norm_ffn, w_up, w_down):
    depth = w_in.shape[0]
    bp, seq, _ = x_prompt.shape
    bs, t_new, _ = x_sample.shape
    wb = cache_swa_k.shape[2]

    w_in_bf = w_in.astype(BF16)
    w_branch_bf = w_branch.astype(BF16)
    w_o_bf = w_o.astype(BF16)
    w_up_bf = w_up.astype(BF16)
    w_down_bf = w_down.astype(BF16)
    pool_w_bf = pool_w.astype(BF16)
    norm_mix3 = norm_mix.reshape(depth, 1, D_MODEL)
    norm_ffn3 = norm_ffn.reshape(depth, 1, D_MODEL)
    onorm3 = hgrn_onorm.reshape(depth, 1, HG_DV)
    pool_scale3 = pool_scale.reshape(depth, 1, POOL_WIDTH)
    lower3 = _lower_bounds(hgrn_lb).reshape(depth, 1, HG_K)
    cache_k4 = cache_swa_k.reshape(depth, bs, wb, SWA_KV)
    cache_v4 = cache_swa_v.reshape(depth, bs, wb, SWA_KV)

    cos_p, sin_p = _rope_tables(jnp.arange(seq, dtype=F32))
    cos_s, sin_s = _rope_tables(jnp.arange(t_new, dtype=F32) + PAST_LEN)
    cos_s = jnp.tile(cos_s, (SAMPLE_BB, 1))
    sin_s = jnp.tile(sin_s, (SAMPLE_BB, 1))
    lane = jnp.arange(256)
    bd = (lane[:, None] // SWA_HEAD_DIM == lane[None, :] // SWA_HEAD_DIM).astype(BF16)
    tok = jnp.arange(HG_CHUNK)
    tri = (tok[:, None] >= tok[None, :]).astype(BF16)

    hp = x_prompt.reshape(bp * seq, D_MODEL)
    hs = x_sample.reshape(bs * t_new, D_MODEL)
    tm_p = _row_tile(bp * seq, 512)
    tm_s = _row_tile(bs * t_new, 512)

    sp = ([], [], [], [])
    ss = ([], [], [], [])
    for l in range(depth):
        qn = jnp.tile(q_norm[l], SWA_HEADS).reshape(1, SWA_Q)
        kn = jnp.tile(k_norm[l], SWA_KV_HEADS).reshape(1, SWA_KV)

        qkv, hq, hf, hi, hg, u, zg = _proj_in(hp, norm_mix3, w_in_bf, l, tm_p)
        o_b, k_last = _swa_prompt(qkv, cos_p, sin_p, qn, kn, bd, attn_sinks, l, bp, seq)
        o_a, s_new = _hgrn_prompt(hq, hf, hi, hg, lower3, onorm3, tri, l, bp, seq)
        o_c = _pool_prompt(u, pool_w_bf, pool_scale3, l, bp, seq)
        x1 = _merge(hp, o_a, o_b, o_c, zg, w_branch_bf, w_o_bf, l, tm_p)
        hp = _ffn(x1, norm_ffn3, w_up_bf, w_down_bf, l, tm_p)
        sp[0].append(s_new)
        sp[1].append(k_last.reshape(bp, WINDOW, SWA_KV_HEADS, SWA_HEAD_DIM))
        v_all = qkv.reshape(bp, seq, QKV_W)[:, seq - WINDOW:, SWA_Q + SWA_KV:]
        sp[2].append(v_all.reshape(bp, WINDOW, SWA_KV_HEADS, SWA_HEAD_DIM))
        sp[3].append(u.reshape(bp, seq, POOL_WIDTH)[:, seq - POOL_BUF:])

        qkv, hq, hf, hi, hg, u, zg = _proj_in(hs, norm_mix3, w_in_bf, l, tm_s)
        o_b, k_rot = _swa_sample(qkv, cos_s, sin_s, qn, kn, bd, attn_sinks, cache_k4, cache_v4, l, bs, t_new)
        o_a, s_new = _hgrn_sample(hq, hf, hi, hg, lower3, onorm3, state_hgrn, l, bs, t_new)
        u3 = u.reshape(bs, t_new, POOL_WIDTH)
        o_c = _pool_sample(jnp.swapaxes(state_pool[l], 0, 1), jnp.swapaxes(u3, 0, 1),
                           pool_w_bf, pool_scale3, l, t_new)
        o_c = jnp.swapaxes(o_c, 0, 1).reshape(bs * t_new, POOL_WIDTH)
        x1 = _merge(hs, o_a, o_b, o_c, zg, w_branch_bf, w_o_bf, l, tm_s)
        hs = _ffn(x1, norm_ffn3, w_up_bf, w_down_bf, l, tm_s)
        v_new = qkv[:, SWA_Q + SWA_KV:].reshape(bs, t_new, SWA_KV)
        kk = jnp.concatenate([cache_k4[l], k_rot.reshape(bs, t_new, SWA_KV)], axis=1)[:, -wb:]
        vv = jnp.concatenate([cache_v4[l], v_new], axis=1)[:, -wb:]
        ss[0].append(s_new)
        ss[1].append(kk.reshape(bs, wb, SWA_KV_HEADS, SWA_HEAD_DIM))
        ss[2].append(vv.reshape(bs, wb, SWA_KV_HEADS, SWA_HEAD_DIM))
        ss[3].append(jnp.concatenate([state_pool[l], u3], axis=1)[:, -POOL_BUF:])

    return (hp.reshape(bp, seq, D_MODEL), hs.reshape(bs, t_new, D_MODEL),
            jnp.stack(sp[0]), jnp.stack(sp[1]), jnp.stack(sp[2]), jnp.stack(sp[3]),
            jnp.stack(ss[0]), jnp.stack(ss[1]), jnp.stack(ss[2]), jnp.stack(ss[3]))
```

```python
import functools

import jax
import jax.numpy as jnp
from jax import lax
from jax.experimental import pallas as pl
from jax.experimental.pallas import tpu as pltpu

F32 = jnp.float32
BF16 = jnp.bfloat16

D_MODEL = 1024
PAST_LEN = 8192
SWA_HEADS = 8
SWA_KV_HEADS = 2
SWA_HEAD_DIM = 64
SWA_GROUP = SWA_HEADS // SWA_KV_HEADS
WINDOW = 128
ROPE_THETA = 10000.0
HG_HEADS = 4
HG_DK = 128
HG_DV = 128
LB_FLOOR = 1e-30
POOL_WINDOWS = (2, 4, 8, 16)
POOL_GROUP_DIM = 128
POOL_WIDTH = 512
POOL_BUF = 15
BRANCH_WIDTH = 512
N_BRANCH = 3
D_FF = 4 * D_MODEL
EPS = 1e-6
NEG = -1e30

SWA_Q = SWA_HEADS * SWA_HEAD_DIM
SWA_KV = SWA_KV_HEADS * SWA_HEAD_DIM
HG_K = HG_HEADS * HG_DK
QKV_W = SWA_Q + 2 * SWA_KV
P3 = QKV_W
P4 = P3 + HG_K
P5 = P4 + HG_K
P6 = P5 + HG_K
P7 = P6 + HG_K
P8 = P7 + POOL_WIDTH
D_IN = P8 + N_BRANCH * D_MODEL

VMEM_LIMIT_BYTES = 56 * 1024 * 1024
HG_CHUNK = 128
HG_DIAG = 8
POOL_TILE = 256
POOL_CARRY = 16
SAMPLE_BB = 8
SWA_BPS = 2
SWA_SAMPLE_BB = 16


def _cparams(*sem):
    return pltpu.CompilerParams(dimension_semantics=sem, vmem_limit_bytes=VMEM_LIMIT_BYTES)


def _rms_rows(x, w):
    ms = jnp.mean(x * x, axis=-1, keepdims=True)
    return x * lax.rsqrt(ms + EPS) * w


def _sigmoid_pair(z):
    e = jnp.exp(-jnp.abs(z))
    r = 1.0 / (1.0 + e)
    er = e * r
    pos = z >= 0
    return jnp.where(pos, r, er), jnp.where(pos, er, r)


def _silu(z):
    return z * _sigmoid_pair(z)[0]


def _split3(x):
    hi = x.astype(BF16)
    r1 = x - hi.astype(F32)
    mid = r1.astype(BF16)
    lo = (r1 - mid.astype(F32)).astype(BF16)
    return hi, mid, lo


def _dot(a, b):
    return jnp.dot(a, b, preferred_element_type=F32)


def _dot_nt(a, b):
    return lax.dot_general(a, b, (((1,), (1,)), ((), ())), preferred_element_type=F32)


def _dot_tn(a, b):
    return lax.dot_general(a, b, (((0,), (0,)), ((), ())), preferred_element_type=F32)


def _lower_kernel(lb_ref, o_ref):
    x = lb_ref[...]
    depth = x.shape[0]
    rows = [x[i:i + 1] for i in range(depth)]
    m = rows[0]
    for r in rows[1:]:
        m = jnp.maximum(m, r)
    es = [jnp.exp(r - m) for r in rows]
    tot = es[0]
    for e in es[1:]:
        tot = tot + e
    ps = [e / tot for e in es]
    out = []
    c = None
    for p in ps:
        c = p if c is None else c + p
        out.append(jnp.maximum(c - ps[0], 0.0))
    o_ref[...] = jnp.concatenate(out, axis=0)


def _lower_bounds(hgrn_lb):
    return pl.pallas_call(
        _lower_kernel,
        out_shape=jax.ShapeDtypeStruct(hgrn_lb.shape, F32),
        name="lower_bounds",
    )(hgrn_lb)


def _proj_in_kernel(x_ref, nw_ref, w_ref, qkv_ref, hq_ref, hf_ref, hi_ref, hg_ref, u_ref, zg_ref):
    xn = _rms_rows(x_ref[...], nw_ref[...]).astype(BF16)

    def mm(a, b):
        return _dot(xn, w_ref[:, a:b])

    qkv_ref[...] = mm(0, P3)
    hq_ref[...] = mm(P3, P4)
    hf_ref[...] = mm(P4, P5)
    hi_ref[...] = mm(P5, P6)
    hg_ref[...] = mm(P6, P7)
    u_ref[...] = mm(P7, P8)
    for n in range(N_BRANCH):
        zg_ref[:, n * D_MODEL:(n + 1) * D_MODEL] = mm(P8 + n * D_MODEL, P8 + (n + 1) * D_MODEL)


def _proj_in(x, norm_w, w_in_bf, layer, tm):
    m = x.shape[0]
    widths = (QKV_W, HG_K, HG_K, HG_K, HG_K, POOL_WIDTH, N_BRANCH * D_MODEL)
    return pl.pallas_call(
        _proj_in_kernel,
        grid=(m // tm,),
        in_specs=[
            pl.BlockSpec((tm, D_MODEL), lambda i: (i, 0)),
            pl.BlockSpec((None, 1, D_MODEL), lambda i: (layer, 0, 0)),
            pl.BlockSpec((None, D_MODEL, D_IN), lambda i: (layer, 0, 0), pipeline_mode=pl.Buffered(1)),
        ],
        out_specs=[pl.BlockSpec((tm, w), lambda i: (i, 0)) for w in widths],
        out_shape=[jax.ShapeDtypeStruct((m, w), F32) for w in widths],
        compiler_params=_cparams("arbitrary"),
        name="proj_in",
    )(x, norm_w, w_in_bf)


def _head_norm_rope(x, w_row, cos, sin, bd):
    width = x.shape[1]
    sq = x * x
    hi = sq.astype(BF16)
    lo = (sq - hi.astype(F32)).astype(BF16)
    parts = []
    for c in range(0, width, 256):
        wc = min(256, width - c)
        b = bd[:wc, :wc]
        parts.append(_dot(hi[:, c:c + wc], b) + _dot(lo[:, c:c + wc], b))
    ms = (parts[0] if len(parts) == 1 else jnp.concatenate(parts, axis=1)) * (1.0 / SWA_HEAD_DIM)
    y = x * lax.rsqrt(ms + EPS) * w_row
    reps = width // 128
    cos_t = cos if reps == 1 else jnp.concatenate([cos] * reps, axis=1)
    sin_t = sin if reps == 1 else jnp.concatenate([sin] * reps, axis=1)
    lane = lax.broadcasted_iota(jnp.int32, y.shape, 1)
    first_half = (lane & (SWA_HEAD_DIM - 1)) < (SWA_HEAD_DIM // 2)
    half = SWA_HEAD_DIM // 2
    rot = jnp.where(first_half, pltpu.roll(y, width - half, 1), pltpu.roll(y, half, 1))
    return y * cos_t + rot * sin_t


def _kv_pad(x):
    lo = lax.broadcasted_iota(jnp.int32, x.shape, 1) < SWA_HEAD_DIM
    xr = pltpu.roll(x, SWA_HEAD_DIM, 1)
    return (jnp.where(lo, x, 0.0), jnp.where(lo, 0.0, xr), jnp.where(lo, xr, 0.0), jnp.where(lo, 0.0, x))


def _sink_softmax_rows(sm, sink):
    mx = jnp.maximum(jnp.max(sm, axis=-1, keepdims=True), sink)
    e = jnp.exp(sm - mx)
    den = jnp.sum(e, axis=-1, keepdims=True) + jnp.exp(sink - mx)
    return e * (1.0 / den)


def _swa_prompt_kernel(sinks_ref, qkv_ref, cos_ref, sin_ref, qn_ref, kn_ref, bd_ref, bias_ref,
                       o_ref, kc_ref, kprep, vprep, *, layer):
    n = pl.program_id(1)

    @pl.when(n == 0)
    def _():
        kprep[...] = jnp.zeros_like(kprep)
        vprep[...] = jnp.zeros_like(vprep)

    cos = cos_ref[...]
    sin = sin_ref[...]
    bd = bd_ref[...]
    bias = bias_ref[...]
    scale = SWA_HEAD_DIM ** -0.5
    for i in range(qkv_ref.shape[0]):
        z = qkv_ref[i]
        q = _head_norm_rope(z[:, :SWA_Q], qn_ref[...], cos, sin, bd) * scale
        k = _head_norm_rope(z[:, SWA_Q:SWA_Q + SWA_KV], kn_ref[...], cos, sin, bd)
        v = z[:, SWA_Q + SWA_KV:]
        kc = [a.astype(BF16) for a in _kv_pad(k)]
        vc = [a.astype(BF16) for a in _kv_pad(v)]
        outs = []
        for kvh in range(SWA_KV_HEADS):
            a, b = 2 * kvh, 2 * kvh + 1
            keys = jnp.concatenate([kprep[i, a], kc[a], kprep[i, b], kc[b]], axis=0)
            vals = jnp.concatenate([vprep[i, a], vc[a], vprep[i, b], vc[b]], axis=0)
            qq = jnp.concatenate([q[:, a * 128:(a + 1) * 128], q[:, b * 128:(b + 1) * 128]], axis=0)
            s = _dot_nt(qq.astype(BF16), keys)
            ps = []
            for half in range(2):
                quads = []
                for r in range(2):
                    sm = s[r * WINDOW:(r + 1) * WINDOW, half * 2 * WINDOW:(half + 1) * 2 * WINDOW] + bias
                    sink = sinks_ref[layer, SWA_GROUP * kvh + 2 * r + half]
                    quads.append(_sink_softmax_rows(sm, sink).astype(BF16))
                ps.append(jnp.concatenate(quads, axis=0))
            og = _dot(jnp.concatenate(ps, axis=1), vals)
            outs.extend([og[:WINDOW], og[WINDOW:]])
        o_ref[i] = jnp.concatenate(outs, axis=1)
        kc_ref[i] = k
        for a in range(2 * SWA_KV_HEADS):
            kprep[i, a] = kc[a]
            vprep[i, a] = vc[a]


def _swa_prompt(qkv, cos, sin, qn, kn, bd, bias, sinks, layer, batch, seq):
    nb = seq // WINDOW
    bps = SWA_BPS if batch % SWA_BPS == 0 else 1
    const = lambda b, n: (0, 0)
    return pl.pallas_call(
        functools.partial(_swa_prompt_kernel, layer=layer),
        grid=(batch // bps, nb),
        in_specs=[
            pl.BlockSpec(memory_space=pltpu.SMEM),
            pl.BlockSpec((bps, WINDOW, QKV_W), lambda b, n: (b, n, 0)),
            pl.BlockSpec((WINDOW, 128), lambda b, n: (n, 0)),
            pl.BlockSpec((WINDOW, 128), lambda b, n: (n, 0)),
            pl.BlockSpec((1, SWA_Q), const),
            pl.BlockSpec((1, SWA_KV), const),
            pl.BlockSpec((256, 256), const),
            pl.BlockSpec((None, WINDOW, 2 * WINDOW), lambda b, n: (jnp.minimum(n, 1), 0, 0)),
        ],
        out_specs=[
            pl.BlockSpec((bps, WINDOW, SWA_Q), lambda b, n: (b, n, 0)),
            pl.BlockSpec((bps, WINDOW, SWA_KV), lambda b, n: (b, 0, 0)),
        ],
        out_shape=[
            jax.ShapeDtypeStruct((batch, seq, SWA_Q), F32),
            jax.ShapeDtypeStruct((batch, WINDOW, SWA_KV), F32),
        ],
        scratch_shapes=[pltpu.VMEM((bps, 2 * SWA_KV_HEADS, WINDOW, 128), BF16),
                        pltpu.VMEM((bps, 2 * SWA_KV_HEADS, WINDOW, 128), BF16)],
        compiler_params=_cparams("arbitrary", "arbitrary"),
        name="swa_prompt",
    )(sinks, qkv.reshape(batch, seq, QKV_W), cos, sin, qn, kn, bd, bias)


def _swa_sample_kernel(sinks_ref, qkv_ref, cos_ref, sin_ref, qn_ref, kn_ref, bd_ref, bias_ref, kb_ref, vb_ref,
                       o_ref, kn_out_ref, *, layer, t_new):
    z = qkv_ref[...]
    cos = cos_ref[...]
    sin = sin_ref[...]
    bd = bd_ref[...]
    q = _head_norm_rope(z[:, :SWA_Q], qn_ref[...], cos, sin, bd) * (SWA_HEAD_DIM ** -0.5)
    k = _head_norm_rope(z[:, SWA_Q:SWA_Q + SWA_KV], kn_ref[...], cos, sin, bd)
    v = z[:, SWA_Q + SWA_KV:]
    kn_out_ref[...] = k

    rows = z.shape[0]
    ng = rows // 8
    nkeys = bias_ref.shape[1]
    wb = kb_ref.shape[1]
    lo = lax.broadcasted_iota(jnp.int32, (rows, 128), 1) < SWA_HEAD_DIM
    qw = []
    for h in range(SWA_HEADS):
        qc = q[:, (h // 2) * 128:(h // 2 + 1) * 128]
        want_hi = h // SWA_GROUP == 1
        if (h % 2 == 1) != want_hi:
            qc = pltpu.roll(qc, SWA_HEAD_DIM, 1)
        qw.append((jnp.where(lo, 0.0, qc) if want_hi else jnp.where(lo, qc, 0.0)).reshape(ng, 8, 128))
    qw = jnp.concatenate(qw, axis=1).astype(BF16)

    def keys_of(cache_ref, new):
        cache = cache_ref[...].reshape(ng, (8 // t_new) * wb, SWA_KV)
        pad = jnp.zeros((ng, nkeys - cache.shape[1] - 8, SWA_KV), F32)
        return jnp.concatenate([cache, new.reshape(ng, 8, SWA_KV), pad], axis=1).astype(BF16)

    s = jnp.einsum('gqd,gkd->gqk', qw, keys_of(kb_ref, k), preferred_element_type=F32)
    bias = bias_ref[...]
    p = jnp.concatenate(
        [_sink_softmax_rows(s[:, h * 8:(h + 1) * 8, :] + bias, sinks_ref[layer, h]).astype(BF16)
         for h in range(SWA_HEADS)], axis=1)
    res = jnp.einsum('gqk,gkd->gqd', p, keys_of(vb_ref, v), preferred_element_type=F32)
    res = res.reshape(ng * SWA_HEADS * 8, 128)
    res_r = pltpu.roll(res, SWA_HEAD_DIM, 1)
    res = res.reshape(ng, SWA_HEADS * 8, 128)
    res_r = res_r.reshape(ng, SWA_HEADS * 8, 128)
    lo3 = lo.reshape(ng, 8, 128)
    cols = []
    for c in range(SWA_HEADS // 2):
        kv_hi = (2 * c) // SWA_GROUP == 1
        even = (res_r if kv_hi else res)[:, (2 * c) * 8:(2 * c + 1) * 8, :]
        odd = (res if kv_hi else res_r)[:, (2 * c + 1) * 8:(2 * c + 2) * 8, :]
        cols.append(jnp.where(lo3, even, odd).reshape(rows, 128))
    o_ref[...] = jnp.concatenate(cols, axis=1)


def _swa_sample(qkv, cos, sin, qn, kn, bd, bias, sinks, kbuf, vbuf, layer, batch, t_new):
    bb = SWA_SAMPLE_BB if batch % SWA_SAMPLE_BB == 0 else batch
    rb = bb * t_new
    wb = kbuf.shape[2]
    row = lambda i: (i, 0)
    const = lambda i: (0, 0)
    return pl.pallas_call(
        functools.partial(_swa_sample_kernel, layer=layer, t_new=t_new),
        grid=(batch // bb,),
        in_specs=[
            pl.BlockSpec(memory_space=pltpu.SMEM),
            pl.BlockSpec((rb, QKV_W), row),
            pl.BlockSpec((rb, 128), const),
            pl.BlockSpec((rb, 128), const),
            pl.BlockSpec((1, SWA_Q), const),
            pl.BlockSpec((1, SWA_KV), const),
            pl.BlockSpec((256, 256), const),
            pl.BlockSpec(bias.shape, const),
            pl.BlockSpec((None, bb, wb, SWA_KV), lambda i: (layer, i, 0, 0)),
            pl.BlockSpec((None, bb, wb, SWA_KV), lambda i: (layer, i, 0, 0)),
        ],
        out_specs=[pl.BlockSpec((rb, SWA_Q), row), pl.BlockSpec((rb, SWA_KV), row)],
        out_shape=[
            jax.ShapeDtypeStruct((batch * t_new, SWA_Q), F32),
            jax.ShapeDtypeStruct((batch * t_new, SWA_KV), F32),
        ],
        compiler_params=_cparams("arbitrary"),
        name="swa_sample",
    )(sinks, qkv, cos, sin, qn, kn, bd, bias, kbuf, vbuf)


def _hgrn_gates(hq, hf, lb):
    sig_p, sig_n = _sigmoid_pair(hf)
    om = 1.0 - lb
    logf = jnp.log(jnp.maximum(lb, LB_FLOOR) + om * sig_p)
    return _silu(hq), logf, om * sig_n


def _hgrn_out(o, onorm, hg):
    outs = []
    for h in range(HG_HEADS):
        hs = slice(h * HG_DV, (h + 1) * HG_DV)
        outs.append(_rms_rows(o[h], onorm) * _silu(hg[:, hs]))
    return jnp.concatenate(outs, axis=1)


def _col_bcast(row):
    return jnp.transpose(jnp.broadcast_to(row, (row.shape[1], row.shape[1])))


def _hgrn_prompt_kernel(hq_ref, hf_ref, hi_ref, hg_ref, lb_ref, on_ref, tri_ref,
                        o_ref, s_out_ref, s_scr):
    n = pl.program_id(1)

    @pl.when(n == 0)
    def _():
        s_scr[...] = jnp.zeros_like(s_scr)

    c = HG_CHUNK
    q, g, kg = _hgrn_gates(hq_ref[...], hf_ref[...], lb_ref[...])
    v = hi_ref[...]
    tri = tri_ref[...]
    g_hi, g_mid, g_lo = _split3(g)
    G = _dot(tri, g_hi) + _dot(tri, g_mid) + _dot(tri, g_lo)

    t_col = lax.broadcasted_iota(jnp.int32, (c, 1), 0)
    ti = lax.broadcasted_iota(jnp.int32, (c, c), 0)
    si = lax.broadcasted_iota(jnp.int32, (c, c), 1)

    levels = []
    half = HG_DIAG
    while half < c:
        blk = 2 * half
        gb = jnp.concatenate(
            [jnp.broadcast_to(G[p * blk + half - 1:p * blk + half, :], (blk, G.shape[1]))
             for p in range(c // blk)], axis=0)
        second = (t_col & half) != 0
        e = jnp.exp(jnp.minimum(jnp.where(second, G - gb, gb - G), 0.0))
        qt = jnp.where(second, q * e, 0.0).astype(BF16)
        kt = jnp.where(second, 0.0, kg * e).astype(BF16)
        shift = blk.bit_length() - 1
        levels.append((qt, kt, (ti >> shift) == (si >> shift)))
        half = blk

    g_last = G[c - 1:c, :]
    q_in = (q * jnp.exp(G)).astype(BF16)
    k_out = (kg * jnp.exp(jnp.minimum(g_last - G, 0.0))).astype(BF16)
    dec_last = jnp.exp(g_last)
    v_bf = v.astype(BF16)

    nblk = c // HG_DIAG
    row_in_blk = lax.broadcasted_iota(jnp.int32, (nblk, HG_DIAG, HG_DV), 1)
    o_heads = []
    for h in range(HG_HEADS):
        hs = slice(h * HG_DK, (h + 1) * HG_DK)
        s_h = s_scr[h]
        a_off = jnp.zeros((c, c), F32)
        for qt, kt, same in levels:
            a_off = a_off + jnp.where(same, _dot_nt(qt[:, hs], kt[:, hs]), 0.0)
        o_h = _dot(q_in[:, hs], s_h.astype(BF16)) + _dot(a_off.astype(BF16), v_bf[:, hs])
        q3 = q[:, hs].reshape(nblk, HG_DIAG, HG_DK)
        k3 = kg[:, hs].reshape(nblk, HG_DIAG, HG_DK)
        g3 = G[:, hs].reshape(nblk, HG_DIAG, HG_DK)
        v3 = v[:, hs].reshape(nblk, HG_DIAG, HG_DV)
        od = jnp.zeros((nblk, HG_DIAG, HG_DV), F32)
        for s in range(HG_DIAG):
            e = jnp.exp(jnp.minimum(g3 - g3[:, s:s + 1, :], 0.0))
            a = jnp.sum(q3 * (k3[:, s:s + 1, :] * e), axis=-1, keepdims=True)
            od = od + a * jnp.where(row_in_blk >= s, v3[:, s:s + 1, :], 0.0)
        o_heads.append(o_h + od.reshape(c, HG_DV))
        s_new = _col_bcast(dec_last[:, hs]) * s_h + _dot_tn(k_out[:, hs], v_bf[:, hs])
        s_scr[h] = s_new
        s_out_ref[h] = s_new
    o_ref[...] = _hgrn_out(o_heads, on_ref[...], hg_ref[...])


def _hgrn_prompt(hq, hf, hi, hg, lower, onorm, tri, layer, batch, seq):
    nc = seq // HG_CHUNK
    row = lambda b, n: (b * nc + n, 0)
    blk = pl.BlockSpec((HG_CHUNK, HG_K), row)
    return pl.pallas_call(
        _hgrn_prompt_kernel,
        grid=(batch, nc),
        in_specs=[
            blk, blk, blk, blk,
            pl.BlockSpec((None, 1, HG_K), lambda b, n: (layer, 0, 0)),
            pl.BlockSpec((None, 1, HG_DV), lambda b, n: (layer, 0, 0)),
            pl.BlockSpec((HG_CHUNK, HG_CHUNK), lambda b, n: (0, 0)),
        ],
        out_specs=[blk, pl.BlockSpec((None, HG_HEADS, HG_DK, HG_DV), lambda b, n: (b, 0, 0, 0))],
        out_shape=[
            jax.ShapeDtypeStruct((batch * seq, HG_K), F32),
            jax.ShapeDtypeStruct((batch, HG_HEADS, HG_DK, HG_DV), F32),
        ],
        scratch_shapes=[pltpu.VMEM((HG_HEADS, HG_DK, HG_DV), F32)],
        compiler_params=_cparams("arbitrary", "arbitrary"),
        name="hgrn_prompt",
    )(hq, hf, hi, hg, lower, onorm, tri)


def _hgrn_sample_kernel(hq_ref, hf_ref, hi_ref, hg_ref, lb_ref, on_ref, s0_ref,
                        o_ref, s_out_ref, *, t_new):
    q, g, kg = _hgrn_gates(hq_ref[...], hf_ref[...], lb_ref[...])
    v = hi_ref[...]
    rows = q.shape[0]
    t_idx = lax.broadcasted_iota(jnp.int32, (rows, 1), 0) % t_new

    def down(x, d):
        return pltpu.roll(x, d, 0)

    def up(x, d):
        return pltpu.roll(x, rows - d, 0)

    G = g
    for d in range(1, t_new):
        G = G + jnp.where(t_idx >= d, down(g, d), 0.0)
    rest = jnp.zeros_like(g)
    for d in range(1, t_new):
        rest = rest + jnp.where(t_idx + d < t_new, up(g, d), 0.0)
    q_in = (q * jnp.exp(G)).astype(BF16)
    k_out = kg * jnp.exp(rest)
    dec_all = jnp.exp(G + rest)

    o_intra = [jnp.zeros((rows, HG_DV), F32) for _ in range(HG_HEADS)]
    for d in range(t_new):
        kd = kg if d == 0 else down(kg, d)
        vd = v if d == 0 else down(v, d)
        gd = G if d == 0 else down(G, d)
        w = q * kd * jnp.exp(jnp.minimum(G - gd, 0.0))
        for h in range(HG_HEADS):
            hs = slice(h * HG_DK, (h + 1) * HG_DK)
            a = jnp.sum(w[:, hs], axis=-1, keepdims=True)
            o_intra[h] = o_intra[h] + jnp.where(t_idx >= d, a, 0.0) * vd[:, hs]

    pad = jnp.zeros((HG_DK - 2 * rows, HG_DK), F32)
    o_heads = [[] for _ in range(HG_HEADS)]
    for h in range(HG_HEADS):
        hs = slice(h * HG_DK, (h + 1) * HG_DK)
        cols = jnp.transpose(jnp.concatenate([k_out[:, hs], dec_all[:, hs], pad], axis=0))
        for b in range(SAMPLE_BB):
            r0 = b * t_new
            s_h = s0_ref[b, h]
            o_heads[h].append(_dot(q_in[r0:r0 + t_new, hs], s_h.astype(BF16)))
            s_new = cols[:, rows + r0:rows + r0 + 1] * s_h
            for s in range(t_new):
                s_new = s_new + cols[:, r0 + s:r0 + s + 1] * v[r0 + s:r0 + s + 1, hs]
            s_out_ref[b, h] = s_new
    o = [jnp.concatenate(o_heads[h], axis=0) + o_intra[h] for h in range(HG_HEADS)]
    o_ref[...] = _hgrn_out(o, on_ref[...], hg_ref[...])


def _hgrn_sample(hq, hf, hi, hg, lower, onorm, state, layer, batch, t_new):
    rb = SAMPLE_BB * t_new
    blk = pl.BlockSpec((rb, HG_K), lambda i: (i, 0))
    return pl.pallas_call(
        functools.partial(_hgrn_sample_kernel, t_new=t_new),
        grid=(batch // SAMPLE_BB,),
        in_specs=[
            blk, blk, blk, blk,
            pl.BlockSpec((None, 1, HG_K), lambda i: (layer, 0, 0)),
            pl.BlockSpec((None, 1, HG_DV), lambda i: (layer, 0, 0)),
            pl.BlockSpec((None, SAMPLE_BB, HG_HEADS, HG_DK, HG_DV), lambda i: (layer, i, 0, 0, 0)),
        ],
        out_specs=[blk, pl.BlockSpec((SAMPLE_BB, HG_HEADS, HG_DK, HG_DV), lambda i: (i, 0, 0, 0))],
        out_shape=[
            jax.ShapeDtypeStruct((batch * t_new, HG_K), F32),
            jax.ShapeDtypeStruct((batch, HG_HEADS, HG_DK, HG_DV), state.dtype),
        ],
        compiler_params=_cparams("arbitrary"),
        name="hgrn_sample",
    )(hq, hf, hi, hg, lower, onorm, state)


def _pool_project(d_groups, pw_ref, ps_ref):
    outs = []
    for gi, d in enumerate(d_groups):
        cs = slice(gi * POOL_GROUP_DIM, (gi + 1) * POOL_GROUP_DIM)
        outs.append(_dot(d.astype(BF16), pw_ref[gi]) * ps_ref[:, cs])
    return jnp.concatenate(outs, axis=1)


def _pool_prompt_kernel(u_ref, pw_ref, ps_ref, o_ref, carry):
    n = pl.program_id(1)

    @pl.when(n == 0)
    def _():
        carry[...] = jnp.zeros_like(carry)

    tt = u_ref.shape[0]
    u = u_ref[...]
    x = jnp.concatenate([carry[...], u], axis=0)
    sums = []
    cur = x
    shift = 1
    for gi in range(len(POOL_WINDOWS)):
        cur = cur + pltpu.roll(cur, shift, 0)
        sums.append(cur[POOL_CARRY:, :POOL_GROUP_DIM])
        cur = cur[:, POOL_GROUP_DIM:]
        shift *= 2
    pos = n * tt + lax.broadcasted_iota(jnp.int32, (tt, 1), 0)
    d_groups = []
    for gi, w in enumerate(POOL_WINDOWS):
        cs = slice(gi * POOL_GROUP_DIM, (gi + 1) * POOL_GROUP_DIM)
        cnt = jnp.minimum(w, pos + 1).astype(F32)
        d_groups.append(sums[gi] / cnt - u[:, cs])
    o_ref[...] = _pool_project(d_groups, pw_ref, ps_ref)
    carry[...] = u[tt - POOL_CARRY:, :]


def _pool_prompt(u, pool_w_bf, pool_scale, layer, batch, seq):
    nt = seq // POOL_TILE
    row = lambda b, n: (b * nt + n, 0)
    return pl.pallas_call(
        _pool_prompt_kernel,
        grid=(batch, nt),
        in_specs=[
            pl.BlockSpec((POOL_TILE, POOL_WIDTH), row),
            pl.BlockSpec((None, len(POOL_WINDOWS), POOL_GROUP_DIM, POOL_GROUP_DIM), lambda b, n: (layer, 0, 0, 0)),
            pl.BlockSpec((None, 1, POOL_WIDTH), lambda b, n: (layer, 0, 0)),
        ],
        out_specs=pl.BlockSpec((POOL_TILE, POOL_WIDTH), row),
        out_shape=jax.ShapeDtypeStruct((batch * seq, POOL_WIDTH), F32),
        scratch_shapes=[pltpu.VMEM((POOL_CARRY, POOL_WIDTH), F32)],
        compiler_params=_cparams("arbitrary", "arbitrary"),
        name="pool_prompt",
    )(u, pool_w_bf, pool_scale)


def _pool_sample_kernel(pb_ref, u_ref, pw_ref, ps_ref, o_ref, *, t_new):
    def row(i):
        return pb_ref[i] if i < POOL_BUF else u_ref[i - POOL_BUF]

    outs = []
    for t in range(t_new):
        e = POOL_BUF + t
        d_groups = []
        for gi, w in enumerate(POOL_WINDOWS):
            cs = slice(gi * POOL_GROUP_DIM, (gi + 1) * POOL_GROUP_DIM)
            lo = max(e + 1 - w, 0)
            acc = row(lo)[:, cs]
            for i in range(lo + 1, e + 1):
                acc = acc + row(i)[:, cs]
            d_groups.append(acc / float(e + 1 - lo) - u_ref[t][:, cs])
        outs.append(_pool_project(d_groups, pw_ref, ps_ref))
    for t in range(t_new):
        o_ref[t] = outs[t]


def _pool_sample(pbuf_t, u_t, pool_w_bf, pool_scale, layer, t_new):
    batch = u_t.shape[1]
    return pl.pallas_call(
        functools.partial(_pool_sample_kernel, t_new=t_new),
        grid=(1,),
        in_specs=[
            pl.BlockSpec((POOL_BUF, batch, POOL_WIDTH), lambda i: (0, 0, 0)),
            pl.BlockSpec((t_new, batch, POOL_WIDTH), lambda i: (0, 0, 0)),
            pl.BlockSpec((None, len(POOL_WINDOWS), POOL_GROUP_DIM, POOL_GROUP_DIM), lambda i: (layer, 0, 0, 0)),
            pl.BlockSpec((None, 1, POOL_WIDTH), lambda i: (layer, 0, 0)),
        ],
        out_specs=pl.BlockSpec((t_new, batch, POOL_WIDTH), lambda i: (0, 0, 0)),
        out_shape=jax.ShapeDtypeStruct((t_new, batch, POOL_WIDTH), F32),
        compiler_params=_cparams("arbitrary"),
        name="pool_sample",
    )(pbuf_t, u_t, pool_w_bf, pool_scale)


def _merge_kernel(x_ref, oa_ref, ob_ref, oc_ref, zg_ref, wb_ref, wo_ref, o_ref):
    m = None
    for n, br in enumerate((oa_ref, ob_ref, oc_ref)):
        proj = _dot(br[...].astype(BF16), wb_ref[n])
        gate = _sigmoid_pair(zg_ref[:, n * D_MODEL:(n + 1) * D_MODEL])[0]
        m = gate * proj if m is None else m + gate * proj
    o_ref[...] = x_ref[...] + _dot(m.astype(BF16), wo_ref[...])


def _merge(x, oa, ob, oc, zg, w_branch_bf, w_o_bf, layer, tm):
    m = x.shape[0]
    row = lambda i: (i, 0)
    return pl.pallas_call(
        _merge_kernel,
        grid=(m // tm,),
        in_specs=[
            pl.BlockSpec((tm, D_MODEL), row),
            pl.BlockSpec((tm, BRANCH_WIDTH), row),
            pl.BlockSpec((tm, BRANCH_WIDTH), row),
            pl.BlockSpec((tm, BRANCH_WIDTH), row),
            pl.BlockSpec((tm, N_BRANCH * D_MODEL), row),
            pl.BlockSpec((None, N_BRANCH, BRANCH_WIDTH, D_MODEL), lambda i: (layer, 0, 0, 0)),
            pl.BlockSpec((None, D_MODEL, D_MODEL), lambda i: (layer, 0, 0)),
        ],
        out_specs=pl.BlockSpec((tm, D_MODEL), row),
        out_shape=jax.ShapeDtypeStruct((m, D_MODEL), F32),
        compiler_params=_cparams("arbitrary"),
        name="merge",
    )(x, oa, ob, oc, zg, w_branch_bf, w_o_bf)


def _ffn_kernel(x_ref, nw_ref, wu_ref, wd_ref, o_ref):
    x = x_ref[...]
    xn = _rms_rows(x, nw_ref[...]).astype(BF16)
    acc = x
    step = D_MODEL
    for c in range(0, D_FF, step):
        h = _dot(xn, wu_ref[:, c:c + step])
        h = jnp.square(jnp.maximum(h, 0.0)).astype(BF16)
        acc = acc + _dot(h, wd_ref[c:c + step, :])
    o_ref[...] = acc


def _ffn(x, norm_w, w_up_bf, w_down_bf, layer, tm):
    m = x.shape[0]
    row = lambda i: (i, 0)
    return pl.pallas_call(
        _ffn_kernel,
        grid=(m // tm,),
        in_specs=[
            pl.BlockSpec((tm, D_MODEL), row),
            pl.BlockSpec((None, 1, D_MODEL), lambda i: (layer, 0, 0)),
            pl.BlockSpec((None, D_MODEL, D_FF), lambda i: (layer, 0, 0), pipeline_mode=pl.Buffered(1)),
            pl.BlockSpec((None, D_FF, D_MODEL), lambda i: (layer, 0, 0), pipeline_mode=pl.Buffered(1)),
        ],
        out_specs=pl.BlockSpec((tm, D_MODEL), row),
        out_shape=jax.ShapeDtypeStruct((m, D_MODEL), F32),
        compiler_params=_cparams("arbitrary"),
        name="ffn",
    )(x, norm_w, w_up_bf, w_down_bf)


def _rope_tables(pos):
    hd = SWA_HEAD_DIM
    inv = jnp.power(ROPE_THETA, -jnp.arange(0, hd, 2, dtype=F32) / hd)
    ang = pos[:, None] * inv[None, :]
    cos = jnp.cos(ang)
    sin = jnp.sin(ang)
    cos_t = jnp.concatenate([cos, cos, cos, cos], axis=1)
    sin_t = jnp.concatenate([-sin, sin, -sin, sin], axis=1)
    return cos_t, sin_t


def _prompt_bias():
    i = jnp.arange(WINDOW)[:, None]
    j = jnp.arange(2 * WINDOW)[None, :]
    band = (j > i) & (j <= i + WINDOW)
    first = band & (j >= WINDOW)
    return jnp.where(jnp.stack([first, band]), 0.0, NEG).astype(F32)


def _sample_bias(wb, t_new):
    grp = 8 // t_new
    n_cache = grp * wb
    nkeys = -(-(n_cache + 8) // 128) * 128
    r = jnp.arange(8)[:, None]
    seq_r, t = r // t_new, r % t_new
    j = jnp.arange(nkeys)[None, :]
    cache_ok = (j < n_cache) & (j // wb == seq_r) & (wb + t - j % wb < WINDOW)
    jn = j - n_cache
    new_ok = (jn >= 0) & (jn < 8) & (jn // t_new == seq_r) & (jn % t_new <= t)
    return jnp.where(cache_ok | new_ok, 0.0, NEG).astype(F32)


def _row_tile(m, cap):
    t = min(m, cap)
    while m % t:
        t //= 2
    return t


def kernel(x_prompt, x_sample, state_hgrn, cache_swa_k, cache_swa_v, state_pool, norm_mix, w_in, q_norm, k_norm,
           attn_sinks, hgrn_lb, hgrn_onorm, pool_w, pool_scale, w_branch, w_o, norm_ffn, w_up, w_down):
    depth = w_in.shape[0]
    bp, seq, _ = x_prompt.shape
    bs, t_new, _ = x_sample.shape
    wb = cache_swa_k.shape[2]

    w_in_bf = w_in.astype(BF16)
    w_branch_bf = w_branch.astype(BF16)
    w_o_bf = w_o.astype(BF16)
    w_up_bf = w_up.astype(BF16)
    w_down_bf = w_down.astype(BF16)
    pool_w_bf = pool_w.astype(BF16)
    norm_mix3 = norm_mix.reshape(depth, 1, D_MODEL)
    norm_ffn3 = norm_ffn.reshape(depth, 1, D_MODEL)
    onorm3 = hgrn_onorm.reshape(depth, 1, HG_DV)
    pool_scale3 = pool_scale.reshape(depth, 1, POOL_WIDTH)
    lower3 = _lower_bounds(hgrn_lb).reshape(depth, 1, HG_K)
    cache_k4 = cache_swa_k.reshape(depth, bs, wb, SWA_KV)
    cache_v4 = cache_swa_v.reshape(depth, bs, wb, SWA_KV)

    cos_p, sin_p = _rope_tables(jnp.arange(seq, dtype=F32))
    cos_s, sin_s = _rope_tables(jnp.arange(t_new, dtype=F32) + PAST_LEN)
    bb_swa = SWA_SAMPLE_BB if bs % SWA_SAMPLE_BB == 0 else bs
    cos_s = jnp.tile(cos_s, (bb_swa, 1))
    sin_s = jnp.tile(sin_s, (bb_swa, 1))
    bias_p = _prompt_bias()
    bias_s = _sample_bias(wb, t_new)
    lane = jnp.arange(256)
    bd = (lane[:, None] // SWA_HEAD_DIM == lane[None, :] // SWA_HEAD_DIM).astype(BF16)
    tok = jnp.arange(HG_CHUNK)
    tri = (tok[:, None] >= tok[None, :]).astype(BF16)

    hp = x_prompt.reshape(bp * seq, D_MODEL)
    hs = x_sample.reshape(bs * t_new, D_MODEL)
    tm_p = _row_tile(bp * seq, 512)
    tm_s = _row_tile(bs * t_new, 512)

    sp = ([], [], [], [])
    ss = ([], [], [], [])
    for l in range(depth):
        qn = jnp.tile(q_norm[l], SWA_HEADS).reshape(1, SWA_Q)
        kn = jnp.tile(k_norm[l], SWA_KV_HEADS).reshape(1, SWA_KV)

        qkv, hq, hf, hi, hg, u, zg = _proj_in(hp, norm_mix3, w_in_bf, l, tm_p)
        o_b, k_last = _swa_prompt(qkv, cos_p, sin_p, qn, kn, bd, bias_p, attn_sinks, l, bp, seq)
        o_b = o_b.reshape(bp * seq, SWA_Q)
        o_a, s_new = _hgrn_prompt(hq, hf, hi, hg, lower3, onorm3, tri, l, bp, seq)
        o_c = _pool_prompt(u, pool_w_bf, pool_scale3, l, bp, seq)
        x1 = _merge(hp, o_a, o_b, o_c, zg, w_branch_bf, w_o_bf, l, tm_p)
        hp = _ffn(x1, norm_ffn3, w_up_bf, w_down_bf, l, tm_p)
        sp[0].append(s_new)
        sp[1].append(k_last.reshape(bp, WINDOW, SWA_KV_HEADS, SWA_HEAD_DIM))
        v_all = qkv.reshape(bp, seq, QKV_W)[:, seq - WINDOW:, SWA_Q + SWA_KV:]
        sp[2].append(v_all.reshape(bp, WINDOW, SWA_KV_HEADS, SWA_HEAD_DIM))
        sp[3].append(u.reshape(bp, seq, POOL_WIDTH)[:, seq - POOL_BUF:])

        qkv, hq, hf, hi, hg, u, zg = _proj_in(hs, norm_mix3, w_in_bf, l, tm_s)
        o_b, k_rot = _swa_sample(qkv, cos_s, sin_s, qn, kn, bd, bias_s, attn_sinks, cache_k4, cache_v4, l, bs, t_new)
        o_a, s_new = _hgrn_sample(hq, hf, hi, hg, lower3, onorm3, state_hgrn, l, bs, t_new)
        u3 = u.reshape(bs, t_new, POOL_WIDTH)
        o_c = _pool_sample(jnp.swapaxes(state_pool[l], 0, 1), jnp.swapaxes(u3, 0, 1),
                           pool_w_bf, pool_scale3, l, t_new)
        o_c = jnp.swapaxes(o_c, 0, 1).reshape(bs * t_new, POOL_WIDTH)
        x1 = _merge(hs, o_a, o_b, o_c, zg, w_branch_bf, w_o_bf, l, tm_s)
        hs = _ffn(x1, norm_ffn3, w_up_bf, w_down_bf, l, tm_s)
        v_new = qkv[:, SWA_Q + SWA_KV:].reshape(bs, t_new, SWA_KV)
        kk = jnp.concatenate([cache_k4[l], k_rot.reshape(bs, t_new, SWA_KV)], axis=1)[:, -wb:]
        vv = jnp.concatenate([cache_v4[l], v_new], axis=1)[:, -wb:]
        ss[0].append(s_new)
        ss[1].append(kk.reshape(bs, wb, SWA_KV_HEADS, SWA_HEAD_DIM))
        ss[2].append(vv.reshape(bs, wb, SWA_KV_HEADS, SWA_HEAD_DIM))
        ss[3].append(jnp.concatenate([state_pool[l], u3], axis=1)[:, -POOL_BUF:])

    return (hp.reshape(bp, seq, D_MODEL), hs.reshape(bs, t_new, D_MODEL),
            jnp.stack(sp[0]), jnp.stack(sp[1]), jnp.stack(sp[2]), jnp.stack(sp[3]),
            jnp.stack(ss[0]), jnp.stack(ss[1]), jnp.stack(ss[2]), jnp.stack(ss[3]))
```

```python
import functools

import jax
import jax.numpy as jnp
from jax import lax
from jax.experimental import pallas as pl
from jax.experimental.pallas import tpu as pltpu

F32 = jnp.float32
BF16 = jnp.bfloat16

D_MODEL = 1024
PAST_LEN = 8192
SWA_HEADS = 8
SWA_KV_HEADS = 2
SWA_HEAD_DIM = 64
SWA_GROUP = SWA_HEADS // SWA_KV_HEADS
WINDOW = 128
ROPE_THETA = 10000.0
HG_HEADS = 4
HG_DK = 128
HG_DV = 128
LB_FLOOR = 1e-30
POOL_WINDOWS = (2, 4, 8, 16)
POOL_GROUP_DIM = 128
POOL_WIDTH = 512
POOL_BUF = 15
BRANCH_WIDTH = 512
N_BRANCH = 3
D_FF = 4 * D_MODEL
EPS = 1e-6
NEG = -1e30

SWA_Q = SWA_HEADS * SWA_HEAD_DIM
SWA_KV = SWA_KV_HEADS * SWA_HEAD_DIM
HG_K = HG_HEADS * HG_DK
QKV_W = SWA_Q + 2 * SWA_KV
P3 = QKV_W
P4 = P3 + HG_K
P5 = P4 + HG_K
P6 = P5 + HG_K
P7 = P6 + HG_K
P8 = P7 + POOL_WIDTH
D_IN = P8 + N_BRANCH * D_MODEL

VMEM_LIMIT_BYTES = 56 * 1024 * 1024
HG_CHUNK = 128
HG_DIAG = 8
POOL_TILE = 256
POOL_CARRY = 16
SAMPLE_BB = 8
SWA_BPS = 2
MIX_TILE = 256
SWA_SAMPLE_BB = 16


def _cparams(*sem):
    return pltpu.CompilerParams(dimension_semantics=sem, vmem_limit_bytes=VMEM_LIMIT_BYTES)


def _rms_rows(x, w):
    ms = jnp.mean(x * x, axis=-1, keepdims=True)
    return x * lax.rsqrt(ms + EPS) * w


def _sigmoid_pair(z):
    e = jnp.exp(-jnp.abs(z))
    r = 1.0 / (1.0 + e)
    er = e * r
    pos = z >= 0
    return jnp.where(pos, r, er), jnp.where(pos, er, r)


def _silu(z):
    return z * _sigmoid_pair(z)[0]


def _split3(x):
    hi = x.astype(BF16)
    r1 = x - hi.astype(F32)
    mid = r1.astype(BF16)
    lo = (r1 - mid.astype(F32)).astype(BF16)
    return hi, mid, lo


def _dot(a, b):
    return jnp.dot(a, b, preferred_element_type=F32)


def _dot_nt(a, b):
    return lax.dot_general(a, b, (((1,), (1,)), ((), ())), preferred_element_type=F32)


def _dot_tn(a, b):
    return lax.dot_general(a, b, (((0,), (0,)), ((), ())), preferred_element_type=F32)


def _lower_kernel(lb_ref, o_ref):
    x = lb_ref[...]
    depth = x.shape[0]
    rows = [x[i:i + 1] for i in range(depth)]
    m = rows[0]
    for r in rows[1:]:
        m = jnp.maximum(m, r)
    es = [jnp.exp(r - m) for r in rows]
    tot = es[0]
    for e in es[1:]:
        tot = tot + e
    ps = [e / tot for e in es]
    out = []
    c = None
    for p in ps:
        c = p if c is None else c + p
        out.append(jnp.maximum(c - ps[0], 0.0))
    o_ref[...] = jnp.concatenate(out, axis=0)


def _lower_bounds(hgrn_lb):
    return pl.pallas_call(
        _lower_kernel,
        out_shape=jax.ShapeDtypeStruct(hgrn_lb.shape, F32),
        name="lower_bounds",
    )(hgrn_lb)


def _proj_in_kernel(x_ref, nw_ref, w_ref, qkv_ref, hq_ref, hf_ref, hi_ref, hg_ref, u_ref, zg_ref):
    xn = _rms_rows(x_ref[...], nw_ref[...]).astype(BF16)

    def mm(a, b):
        return _dot(xn, w_ref[:, a:b])

    qkv_ref[...] = mm(0, P3)
    hq_ref[...] = mm(P3, P4)
    hf_ref[...] = mm(P4, P5)
    hi_ref[...] = mm(P5, P6)
    hg_ref[...] = mm(P6, P7)
    u_ref[...] = mm(P7, P8)
    for n in range(N_BRANCH):
        zg_ref[:, n * D_MODEL:(n + 1) * D_MODEL] = mm(P8 + n * D_MODEL, P8 + (n + 1) * D_MODEL)


def _proj_in(x, norm_w, w_in_bf, layer, tm):
    m = x.shape[0]
    widths = (QKV_W, HG_K, HG_K, HG_K, HG_K, POOL_WIDTH, N_BRANCH * D_MODEL)
    return pl.pallas_call(
        _proj_in_kernel,
        grid=(m // tm,),
        in_specs=[
            pl.BlockSpec((tm, D_MODEL), lambda i: (i, 0)),
            pl.BlockSpec((None, 1, D_MODEL), lambda i: (layer, 0, 0)),
            pl.BlockSpec((None, D_MODEL, D_IN), lambda i: (layer, 0, 0), pipeline_mode=pl.Buffered(1)),
        ],
        out_specs=[pl.BlockSpec((tm, w), lambda i: (i, 0)) for w in widths],
        out_shape=[jax.ShapeDtypeStruct((m, w), F32) for w in widths],
        compiler_params=_cparams("arbitrary"),
        name="proj_in",
    )(x, norm_w, w_in_bf)


def _head_norm_rope(x, w_row, cos, sin, bd):
    width = x.shape[1]
    sq = x * x
    hi = sq.astype(BF16)
    lo = (sq - hi.astype(F32)).astype(BF16)
    parts = []
    for c in range(0, width, 256):
        wc = min(256, width - c)
        b = bd[:wc, :wc]
        parts.append(_dot(hi[:, c:c + wc], b) + _dot(lo[:, c:c + wc], b))
    ms = (parts[0] if len(parts) == 1 else jnp.concatenate(parts, axis=1)) * (1.0 / SWA_HEAD_DIM)
    y = x * lax.rsqrt(ms + EPS) * w_row
    reps = width // 128
    cos_t = cos if reps == 1 else jnp.concatenate([cos] * reps, axis=1)
    sin_t = sin if reps == 1 else jnp.concatenate([sin] * reps, axis=1)
    lane = lax.broadcasted_iota(jnp.int32, y.shape, 1)
    first_half = (lane & (SWA_HEAD_DIM - 1)) < (SWA_HEAD_DIM // 2)
    half = SWA_HEAD_DIM // 2
    rot = jnp.where(first_half, pltpu.roll(y, width - half, 1), pltpu.roll(y, half, 1))
    return y * cos_t + rot * sin_t


def _kv_pad(x):
    lo = lax.broadcasted_iota(jnp.int32, x.shape, 1) < SWA_HEAD_DIM
    xr = pltpu.roll(x, SWA_HEAD_DIM, 1)
    return (jnp.where(lo, x, 0.0), jnp.where(lo, 0.0, xr), jnp.where(lo, xr, 0.0), jnp.where(lo, 0.0, x))


def _sink_softmax_rows(sm, sink):
    mx = jnp.maximum(jnp.max(sm, axis=-1, keepdims=True), sink)
    e = jnp.exp(sm - mx)
    den = jnp.sum(e, axis=-1, keepdims=True) + jnp.exp(sink - mx)
    return e * (1.0 / den)


def _swa_prompt_kernel(sinks_ref, qkv_ref, cos_ref, sin_ref, qn_ref, kn_ref, bd_ref, bias_ref,
                       o_ref, kc_ref, kprep, vprep, *, layer):
    n = pl.program_id(1)

    @pl.when(n == 0)
    def _():
        kprep[...] = jnp.zeros_like(kprep)
        vprep[...] = jnp.zeros_like(vprep)

    cos = cos_ref[...]
    sin = sin_ref[...]
    bd = bd_ref[...]
    bias = bias_ref[...]
    scale = SWA_HEAD_DIM ** -0.5
    for i in range(qkv_ref.shape[0]):
        z = qkv_ref[i]
        q = _head_norm_rope(z[:, :SWA_Q], qn_ref[...], cos, sin, bd) * scale
        k = _head_norm_rope(z[:, SWA_Q:SWA_Q + SWA_KV], kn_ref[...], cos, sin, bd)
        v = z[:, SWA_Q + SWA_KV:]
        kc = [a.astype(BF16) for a in _kv_pad(k)]
        vc = [a.astype(BF16) for a in _kv_pad(v)]
        outs = []
        for kvh in range(SWA_KV_HEADS):
            a, b = 2 * kvh, 2 * kvh + 1
            keys = jnp.concatenate([kprep[i, a], kc[a], kprep[i, b], kc[b]], axis=0)
            vals = jnp.concatenate([vprep[i, a], vc[a], vprep[i, b], vc[b]], axis=0)
            qq = jnp.concatenate([q[:, a * 128:(a + 1) * 128], q[:, b * 128:(b + 1) * 128]], axis=0)
            s = _dot_nt(qq.astype(BF16), keys)
            ps = []
            for half in range(2):
                quads = []
                for r in range(2):
                    sm = s[r * WINDOW:(r + 1) * WINDOW, half * 2 * WINDOW:(half + 1) * 2 * WINDOW] + bias
                    sink = sinks_ref[layer, SWA_GROUP * kvh + 2 * r + half]
                    quads.append(_sink_softmax_rows(sm, sink).astype(BF16))
                ps.append(jnp.concatenate(quads, axis=0))
            og = _dot(jnp.concatenate(ps, axis=1), vals)
            outs.extend([og[:WINDOW], og[WINDOW:]])
        o_ref[i] = jnp.concatenate(outs, axis=1)
        kc_ref[i] = k
        for a in range(2 * SWA_KV_HEADS):
            kprep[i, a] = kc[a]
            vprep[i, a] = vc[a]


def _swa_prompt(qkv, cos, sin, qn, kn, bd, bias, sinks, layer, batch, seq):
    nb = seq // WINDOW
    bps = SWA_BPS if batch % SWA_BPS == 0 else 1
    const = lambda b, n: (0, 0)
    return pl.pallas_call(
        functools.partial(_swa_prompt_kernel, layer=layer),
        grid=(batch // bps, nb),
        in_specs=[
            pl.BlockSpec(memory_space=pltpu.SMEM),
            pl.BlockSpec((bps, WINDOW, QKV_W), lambda b, n: (b, n, 0)),
            pl.BlockSpec((WINDOW, 128), lambda b, n: (n, 0)),
            pl.BlockSpec((WINDOW, 128), lambda b, n: (n, 0)),
            pl.BlockSpec((1, SWA_Q), const),
            pl.BlockSpec((1, SWA_KV), const),
            pl.BlockSpec((256, 256), const),
            pl.BlockSpec((None, WINDOW, 2 * WINDOW), lambda b, n: (jnp.minimum(n, 1), 0, 0)),
        ],
        out_specs=[
            pl.BlockSpec((bps, WINDOW, SWA_Q), lambda b, n: (b, n, 0)),
            pl.BlockSpec((bps, WINDOW, SWA_KV), lambda b, n: (b, 0, 0)),
        ],
        out_shape=[
            jax.ShapeDtypeStruct((batch, seq, SWA_Q), F32),
            jax.ShapeDtypeStruct((batch, WINDOW, SWA_KV), F32),
        ],
        scratch_shapes=[pltpu.VMEM((bps, 2 * SWA_KV_HEADS, WINDOW, 128), BF16),
                        pltpu.VMEM((bps, 2 * SWA_KV_HEADS, WINDOW, 128), BF16)],
        compiler_params=_cparams("arbitrary", "arbitrary"),
        name="swa_prompt",
    )(sinks, qkv.reshape(batch, seq, QKV_W), cos, sin, qn, kn, bd, bias)


def _swa_sample_kernel(sinks_ref, qkv_ref, cos_ref, sin_ref, qn_ref, kn_ref, bd_ref, bias_ref, kb_ref, vb_ref,
                       o_ref, kn_out_ref, *, layer, t_new):
    z = qkv_ref[...]
    cos = cos_ref[...]
    sin = sin_ref[...]
    bd = bd_ref[...]
    q = _head_norm_rope(z[:, :SWA_Q], qn_ref[...], cos, sin, bd) * (SWA_HEAD_DIM ** -0.5)
    k = _head_norm_rope(z[:, SWA_Q:SWA_Q + SWA_KV], kn_ref[...], cos, sin, bd)
    v = z[:, SWA_Q + SWA_KV:]
    kn_out_ref[...] = k

    rows = z.shape[0]
    ng = rows // 8
    nkeys = bias_ref.shape[1]
    wb = kb_ref.shape[1]
    lo = lax.broadcasted_iota(jnp.int32, (rows, 128), 1) < SWA_HEAD_DIM
    qw = []
    for h in range(SWA_HEADS):
        qc = q[:, (h // 2) * 128:(h // 2 + 1) * 128]
        want_hi = h // SWA_GROUP == 1
        if (h % 2 == 1) != want_hi:
            qc = pltpu.roll(qc, SWA_HEAD_DIM, 1)
        qw.append((jnp.where(lo, 0.0, qc) if want_hi else jnp.where(lo, qc, 0.0)).reshape(ng, 8, 128))
    qw = jnp.concatenate(qw, axis=1).astype(BF16)

    def keys_of(cache_ref, new):
        cache = cache_ref[...].reshape(ng, (8 // t_new) * wb, SWA_KV)
        pad = jnp.zeros((ng, nkeys - cache.shape[1] - 8, SWA_KV), F32)
        return jnp.concatenate([cache, new.reshape(ng, 8, SWA_KV), pad], axis=1).astype(BF16)

    s = jnp.einsum('gqd,gkd->gqk', qw, keys_of(kb_ref, k), preferred_element_type=F32)
    bias = bias_ref[...]
    p = jnp.concatenate(
        [_sink_softmax_rows(s[:, h * 8:(h + 1) * 8, :] + bias, sinks_ref[layer, h]).astype(BF16)
         for h in range(SWA_HEADS)], axis=1)
    res = jnp.einsum('gqk,gkd->gqd', p, keys_of(vb_ref, v), preferred_element_type=F32)
    res = res.reshape(ng * SWA_HEADS * 8, 128)
    res_r = pltpu.roll(res, SWA_HEAD_DIM, 1)
    res = res.reshape(ng, SWA_HEADS * 8, 128)
    res_r = res_r.reshape(ng, SWA_HEADS * 8, 128)
    lo3 = lo.reshape(ng, 8, 128)
    cols = []
    for c in range(SWA_HEADS // 2):
        kv_hi = (2 * c) // SWA_GROUP == 1
        even = (res_r if kv_hi else res)[:, (2 * c) * 8:(2 * c + 1) * 8, :]
        odd = (res if kv_hi else res_r)[:, (2 * c + 1) * 8:(2 * c + 2) * 8, :]
        cols.append(jnp.where(lo3, even, odd).reshape(rows, 128))
    o_ref[...] = jnp.concatenate(cols, axis=1)


def _swa_sample(qkv, cos, sin, qn, kn, bd, bias, sinks, kbuf, vbuf, layer, batch, t_new):
    bb = SWA_SAMPLE_BB if batch % SWA_SAMPLE_BB == 0 else batch
    rb = bb * t_new
    wb = kbuf.shape[2]
    row = lambda i: (i, 0)
    const = lambda i: (0, 0)
    return pl.pallas_call(
        functools.partial(_swa_sample_kernel, layer=layer, t_new=t_new),
        grid=(batch // bb,),
        in_specs=[
            pl.BlockSpec(memory_space=pltpu.SMEM),
            pl.BlockSpec((rb, QKV_W), row),
            pl.BlockSpec((rb, 128), const),
            pl.BlockSpec((rb, 128), const),
            pl.BlockSpec((1, SWA_Q), const),
            pl.BlockSpec((1, SWA_KV), const),
            pl.BlockSpec((256, 256), const),
            pl.BlockSpec(bias.shape, const),
            pl.BlockSpec((None, bb, wb, SWA_KV), lambda i: (layer, i, 0, 0)),
            pl.BlockSpec((None, bb, wb, SWA_KV), lambda i: (layer, i, 0, 0)),
        ],
        out_specs=[pl.BlockSpec((rb, SWA_Q), row), pl.BlockSpec((rb, SWA_KV), row)],
        out_shape=[
            jax.ShapeDtypeStruct((batch * t_new, SWA_Q), F32),
            jax.ShapeDtypeStruct((batch * t_new, SWA_KV), F32),
        ],
        compiler_params=_cparams("arbitrary"),
        name="swa_sample",
    )(sinks, qkv, cos, sin, qn, kn, bd, bias, kbuf, vbuf)


def _hgrn_gates(hq, hf, lb):
    sig_p, sig_n = _sigmoid_pair(hf)
    om = 1.0 - lb
    logf = jnp.log(jnp.maximum(lb, LB_FLOOR) + om * sig_p)
    return _silu(hq), logf, om * sig_n


def _hgrn_out(o, onorm, hg):
    outs = []
    for h in range(HG_HEADS):
        hs = slice(h * HG_DV, (h + 1) * HG_DV)
        outs.append(_rms_rows(o[h], onorm) * _silu(hg[:, hs]))
    return jnp.concatenate(outs, axis=1)


def _col_bcast(row):
    return jnp.transpose(jnp.broadcast_to(row, (row.shape[1], row.shape[1])))


def _hgrn_prompt_kernel(hq_ref, hf_ref, hi_ref, hg_ref, lb_ref, on_ref, tri_ref,
                        o_ref, s_out_ref, s_scr):
    n = pl.program_id(1)

    @pl.when(n == 0)
    def _():
        s_scr[...] = jnp.zeros_like(s_scr)

    c = HG_CHUNK
    q, g, kg = _hgrn_gates(hq_ref[...], hf_ref[...], lb_ref[...])
    v = hi_ref[...]
    tri = tri_ref[...]
    g_hi, g_mid, g_lo = _split3(g)
    G = _dot(tri, g_hi) + _dot(tri, g_mid) + _dot(tri, g_lo)

    t_col = lax.broadcasted_iota(jnp.int32, (c, 1), 0)
    ti = lax.broadcasted_iota(jnp.int32, (c, c), 0)
    si = lax.broadcasted_iota(jnp.int32, (c, c), 1)

    levels = []
    half = HG_DIAG
    while half < c:
        blk = 2 * half
        gb = jnp.concatenate(
            [jnp.broadcast_to(G[p * blk + half - 1:p * blk + half, :], (blk, G.shape[1]))
             for p in range(c // blk)], axis=0)
        second = (t_col & half) != 0
        e = jnp.exp(jnp.minimum(jnp.where(second, G - gb, gb - G), 0.0))
        qt = jnp.where(second, q * e, 0.0).astype(BF16)
        kt = jnp.where(second, 0.0, kg * e).astype(BF16)
        shift = blk.bit_length() - 1
        levels.append((qt, kt, (ti >> shift) == (si >> shift)))
        half = blk

    g_last = G[c - 1:c, :]
    q_in = (q * jnp.exp(G)).astype(BF16)
    k_out = (kg * jnp.exp(jnp.minimum(g_last - G, 0.0))).astype(BF16)
    dec_last = jnp.exp(g_last)
    v_bf = v.astype(BF16)

    nblk = c // HG_DIAG
    row_in_blk = lax.broadcasted_iota(jnp.int32, (nblk, HG_DIAG, HG_DV), 1)
    o_heads = []
    for h in range(HG_HEADS):
        hs = slice(h * HG_DK, (h + 1) * HG_DK)
        s_h = s_scr[h]
        a_off = jnp.zeros((c, c), F32)
        for qt, kt, same in levels:
            a_off = a_off + jnp.where(same, _dot_nt(qt[:, hs], kt[:, hs]), 0.0)
        o_h = _dot(q_in[:, hs], s_h.astype(BF16)) + _dot(a_off.astype(BF16), v_bf[:, hs])
        q3 = q[:, hs].reshape(nblk, HG_DIAG, HG_DK)
        k3 = kg[:, hs].reshape(nblk, HG_DIAG, HG_DK)
        g3 = G[:, hs].reshape(nblk, HG_DIAG, HG_DK)
        v3 = v[:, hs].reshape(nblk, HG_DIAG, HG_DV)
        od = jnp.zeros((nblk, HG_DIAG, HG_DV), F32)
        for s in range(HG_DIAG):
            e = jnp.exp(jnp.minimum(g3 - g3[:, s:s + 1, :], 0.0))
            a = jnp.sum(q3 * (k3[:, s:s + 1, :] * e), axis=-1, keepdims=True)
            od = od + a * jnp.where(row_in_blk >= s, v3[:, s:s + 1, :], 0.0)
        o_heads.append(o_h + od.reshape(c, HG_DV))
        s_new = _col_bcast(dec_last[:, hs]) * s_h + _dot_tn(k_out[:, hs], v_bf[:, hs])
        s_scr[h] = s_new
        s_out_ref[h] = s_new
    o_ref[...] = _hgrn_out(o_heads, on_ref[...], hg_ref[...])


def _hgrn_prompt(hq, hf, hi, hg, lower, onorm, tri, layer, batch, seq):
    nc = seq // HG_CHUNK
    row = lambda b, n: (b * nc + n, 0)
    blk = pl.BlockSpec((HG_CHUNK, HG_K), row)
    return pl.pallas_call(
        _hgrn_prompt_kernel,
        grid=(batch, nc),
        in_specs=[
            blk, blk, blk, blk,
            pl.BlockSpec((None, 1, HG_K), lambda b, n: (layer, 0, 0)),
            pl.BlockSpec((None, 1, HG_DV), lambda b, n: (layer, 0, 0)),
            pl.BlockSpec((HG_CHUNK, HG_CHUNK), lambda b, n: (0, 0)),
        ],
        out_specs=[blk, pl.BlockSpec((None, HG_HEADS, HG_DK, HG_DV), lambda b, n: (b, 0, 0, 0))],
        out_shape=[
            jax.ShapeDtypeStruct((batch * seq, HG_K), F32),
            jax.ShapeDtypeStruct((batch, HG_HEADS, HG_DK, HG_DV), F32),
        ],
        scratch_shapes=[pltpu.VMEM((HG_HEADS, HG_DK, HG_DV), F32)],
        compiler_params=_cparams("arbitrary", "arbitrary"),
        name="hgrn_prompt",
    )(hq, hf, hi, hg, lower, onorm, tri)


def _hgrn_sample_kernel(hq_ref, hf_ref, hi_ref, hg_ref, lb_ref, on_ref, s0_ref,
                        o_ref, s_out_ref, *, t_new):
    q, g, kg = _hgrn_gates(hq_ref[...], hf_ref[...], lb_ref[...])
    v = hi_ref[...]
    rows = q.shape[0]
    t_idx = lax.broadcasted_iota(jnp.int32, (rows, 1), 0) % t_new

    def down(x, d):
        return pltpu.roll(x, d, 0)

    def up(x, d):
        return pltpu.roll(x, rows - d, 0)

    G = g
    for d in range(1, t_new):
        G = G + jnp.where(t_idx >= d, down(g, d), 0.0)
    rest = jnp.zeros_like(g)
    for d in range(1, t_new):
        rest = rest + jnp.where(t_idx + d < t_new, up(g, d), 0.0)
    q_in = (q * jnp.exp(G)).astype(BF16)
    k_out = kg * jnp.exp(rest)
    dec_all = jnp.exp(G + rest)

    o_intra = [jnp.zeros((rows, HG_DV), F32) for _ in range(HG_HEADS)]
    for d in range(t_new):
        kd = kg if d == 0 else down(kg, d)
        vd = v if d == 0 else down(v, d)
        gd = G if d == 0 else down(G, d)
        w = q * kd * jnp.exp(jnp.minimum(G - gd, 0.0))
        for h in range(HG_HEADS):
            hs = slice(h * HG_DK, (h + 1) * HG_DK)
            a = jnp.sum(w[:, hs], axis=-1, keepdims=True)
            o_intra[h] = o_intra[h] + jnp.where(t_idx >= d, a, 0.0) * vd[:, hs]

    pad = jnp.zeros((HG_DK - 2 * rows, HG_DK), F32)
    o_heads = [[] for _ in range(HG_HEADS)]
    for h in range(HG_HEADS):
        hs = slice(h * HG_DK, (h + 1) * HG_DK)
        cols = jnp.transpose(jnp.concatenate([k_out[:, hs], dec_all[:, hs], pad], axis=0))
        for b in range(SAMPLE_BB):
            r0 = b * t_new
            s_h = s0_ref[b, h]
            o_heads[h].append(_dot(q_in[r0:r0 + t_new, hs], s_h.astype(BF16)))
            s_new = cols[:, rows + r0:rows + r0 + 1] * s_h
            for s in range(t_new):
                s_new = s_new + cols[:, r0 + s:r0 + s + 1] * v[r0 + s:r0 + s + 1, hs]
            s_out_ref[b, h] = s_new
    o = [jnp.concatenate(o_heads[h], axis=0) + o_intra[h] for h in range(HG_HEADS)]
    o_ref[...] = _hgrn_out(o, on_ref[...], hg_ref[...])


def _hgrn_sample(hq, hf, hi, hg, lower, onorm, state, layer, batch, t_new):
    rb = SAMPLE_BB * t_new
    blk = pl.BlockSpec((rb, HG_K), lambda i: (i, 0))
    return pl.pallas_call(
        functools.partial(_hgrn_sample_kernel, t_new=t_new),
        grid=(batch // SAMPLE_BB,),
        in_specs=[
            blk, blk, blk, blk,
            pl.BlockSpec((None, 1, HG_K), lambda i: (layer, 0, 0)),
            pl.BlockSpec((None, 1, HG_DV), lambda i: (layer, 0, 0)),
            pl.BlockSpec((None, SAMPLE_BB, HG_HEADS, HG_DK, HG_DV), lambda i: (layer, i, 0, 0, 0)),
        ],
        out_specs=[blk, pl.BlockSpec((SAMPLE_BB, HG_HEADS, HG_DK, HG_DV), lambda i: (i, 0, 0, 0))],
        out_shape=[
            jax.ShapeDtypeStruct((batch * t_new, HG_K), F32),
            jax.ShapeDtypeStruct((batch, HG_HEADS, HG_DK, HG_DV), state.dtype),
        ],
        compiler_params=_cparams("arbitrary"),
        name="hgrn_sample",
    )(hq, hf, hi, hg, lower, onorm, state)


def _pool_project(d_groups, pw_ref, ps_ref):
    outs = []
    for gi, d in enumerate(d_groups):
        cs = slice(gi * POOL_GROUP_DIM, (gi + 1) * POOL_GROUP_DIM)
        outs.append(_dot(d.astype(BF16), pw_ref[gi]) * ps_ref[:, cs])
    return jnp.concatenate(outs, axis=1)


def _pool_prompt_kernel(u_ref, pw_ref, ps_ref, o_ref, carry):
    n = pl.program_id(1)

    @pl.when(n == 0)
    def _():
        carry[...] = jnp.zeros_like(carry)

    tt = u_ref.shape[0]
    u = u_ref[...]
    x = jnp.concatenate([carry[...], u], axis=0)
    sums = []
    cur = x
    shift = 1
    for gi in range(len(POOL_WINDOWS)):
        cur = cur + pltpu.roll(cur, shift, 0)
        sums.append(cur[POOL_CARRY:, :POOL_GROUP_DIM])
        cur = cur[:, POOL_GROUP_DIM:]
        shift *= 2
    pos = n * tt + lax.broadcasted_iota(jnp.int32, (tt, 1), 0)
    d_groups = []
    for gi, w in enumerate(POOL_WINDOWS):
        cs = slice(gi * POOL_GROUP_DIM, (gi + 1) * POOL_GROUP_DIM)
        cnt = jnp.minimum(w, pos + 1).astype(F32)
        d_groups.append(sums[gi] / cnt - u[:, cs])
    o_ref[...] = _pool_project(d_groups, pw_ref, ps_ref)
    carry[...] = u[tt - POOL_CARRY:, :]


def _pool_prompt(u, pool_w_bf, pool_scale, layer, batch, seq):
    nt = seq // POOL_TILE
    row = lambda b, n: (b * nt + n, 0)
    return pl.pallas_call(
        _pool_prompt_kernel,
        grid=(batch, nt),
        in_specs=[
            pl.BlockSpec((POOL_TILE, POOL_WIDTH), row),
            pl.BlockSpec((None, len(POOL_WINDOWS), POOL_GROUP_DIM, POOL_GROUP_DIM), lambda b, n: (layer, 0, 0, 0)),
            pl.BlockSpec((None, 1, POOL_WIDTH), lambda b, n: (layer, 0, 0)),
        ],
        out_specs=pl.BlockSpec((POOL_TILE, POOL_WIDTH), row),
        out_shape=jax.ShapeDtypeStruct((batch * seq, POOL_WIDTH), F32),
        scratch_shapes=[pltpu.VMEM((POOL_CARRY, POOL_WIDTH), F32)],
        compiler_params=_cparams("arbitrary", "arbitrary"),
        name="pool_prompt",
    )(u, pool_w_bf, pool_scale)


def _pool_sample_kernel(pb_ref, u_ref, pw_ref, ps_ref, o_ref, *, t_new):
    def row(i):
        return pb_ref[i] if i < POOL_BUF else u_ref[i - POOL_BUF]

    outs = []
    for t in range(t_new):
        e = POOL_BUF + t
        d_groups = []
        for gi, w in enumerate(POOL_WINDOWS):
            cs = slice(gi * POOL_GROUP_DIM, (gi + 1) * POOL_GROUP_DIM)
            lo = max(e + 1 - w, 0)
            acc = row(lo)[:, cs]
            for i in range(lo + 1, e + 1):
                acc = acc + row(i)[:, cs]
            d_groups.append(acc / float(e + 1 - lo) - u_ref[t][:, cs])
        outs.append(_pool_project(d_groups, pw_ref, ps_ref))
    for t in range(t_new):
        o_ref[t] = outs[t]


def _pool_sample(pbuf_t, u_t, pool_w_bf, pool_scale, layer, t_new):
    batch = u_t.shape[1]
    return pl.pallas_call(
        functools.partial(_pool_sample_kernel, t_new=t_new),
        grid=(1,),
        in_specs=[
            pl.BlockSpec((POOL_BUF, batch, POOL_WIDTH), lambda i: (0, 0, 0)),
            pl.BlockSpec((t_new, batch, POOL_WIDTH), lambda i: (0, 0, 0)),
            pl.BlockSpec((None, len(POOL_WINDOWS), POOL_GROUP_DIM, POOL_GROUP_DIM), lambda i: (layer, 0, 0, 0)),
            pl.BlockSpec((None, 1, POOL_WIDTH), lambda i: (layer, 0, 0)),
        ],
        out_specs=pl.BlockSpec((t_new, batch, POOL_WIDTH), lambda i: (0, 0, 0)),
        out_shape=jax.ShapeDtypeStruct((t_new, batch, POOL_WIDTH), F32),
        compiler_params=_cparams("arbitrary"),
        name="pool_sample",
    )(pbuf_t, u_t, pool_w_bf, pool_scale)


def _merge_kernel(x_ref, oa_ref, ob_ref, oc_ref, zg_ref, wb_ref, wo_ref, o_ref):
    m = None
    for n, br in enumerate((oa_ref, ob_ref, oc_ref)):
        proj = _dot(br[...].astype(BF16), wb_ref[n])
        gate = _sigmoid_pair(zg_ref[:, n * D_MODEL:(n + 1) * D_MODEL])[0]
        m = gate * proj if m is None else m + gate * proj
    o_ref[...] = x_ref[...] + _dot(m.astype(BF16), wo_ref[...])


def _merge(x, oa, ob, oc, zg, w_branch_bf, w_o_bf, layer, tm):
    m = x.shape[0]
    row = lambda i: (i, 0)
    return pl.pallas_call(
        _merge_kernel,
        grid=(m // tm,),
        in_specs=[
            pl.BlockSpec((tm, D_MODEL), row),
            pl.BlockSpec((tm, BRANCH_WIDTH), row),
            pl.BlockSpec((tm, BRANCH_WIDTH), row),
            pl.BlockSpec((tm, BRANCH_WIDTH), row),
            pl.BlockSpec((tm, N_BRANCH * D_MODEL), row),
            pl.BlockSpec((None, N_BRANCH, BRANCH_WIDTH, D_MODEL), lambda i: (layer, 0, 0, 0)),
            pl.BlockSpec((None, D_MODEL, D_MODEL), lambda i: (layer, 0, 0)),
        ],
        out_specs=pl.BlockSpec((tm, D_MODEL), row),
        out_shape=jax.ShapeDtypeStruct((m, D_MODEL), F32),
        compiler_params=_cparams("arbitrary"),
        name="merge",
    )(x, oa, ob, oc, zg, w_branch_bf, w_o_bf)


_Z_GROUPS = ((0, P3), (P3, P7), (P7, P8)) + tuple(
    (P8 + n * D_MODEL, P8 + (n + 1) * D_MODEL) for n in range(N_BRANCH))


def _project_tile(x, nw, w_ref, z_ref):
    xn = _rms_rows(x, nw).astype(BF16)
    for a, b in _Z_GROUPS:
        z_ref[:, a:b] = _dot(xn, w_ref[:, a:b])


def _swa_block(z, cos, sin, qn, kn, bd, bias, sinks_ref, layer, kprep, vprep):
    q = _head_norm_rope(z[:, :SWA_Q], qn, cos, sin, bd) * (SWA_HEAD_DIM ** -0.5)
    k = _head_norm_rope(z[:, SWA_Q:SWA_Q + SWA_KV], kn, cos, sin, bd)
    v = z[:, SWA_Q + SWA_KV:]
    kc = [a.astype(BF16) for a in _kv_pad(k)]
    vc = [a.astype(BF16) for a in _kv_pad(v)]
    outs = []
    for kvh in range(SWA_KV_HEADS):
        a, b = 2 * kvh, 2 * kvh + 1
        keys = jnp.concatenate([kprep[a], kc[a], kprep[b], kc[b]], axis=0)
        vals = jnp.concatenate([vprep[a], vc[a], vprep[b], vc[b]], axis=0)
        qq = jnp.concatenate([q[:, a * 128:(a + 1) * 128], q[:, b * 128:(b + 1) * 128]], axis=0)
        s = _dot_nt(qq.astype(BF16), keys)
        ps = []
        for half in range(2):
            quads = []
            for r in range(2):
                sm = s[r * WINDOW:(r + 1) * WINDOW, half * 2 * WINDOW:(half + 1) * 2 * WINDOW] + bias
                sink = sinks_ref[layer, SWA_GROUP * kvh + 2 * r + half]
                quads.append(_sink_softmax_rows(sm, sink).astype(BF16))
            ps.append(jnp.concatenate(quads, axis=0))
        og = _dot(jnp.concatenate(ps, axis=1), vals)
        outs.extend([og[:WINDOW], og[WINDOW:]])
    for a in range(2 * SWA_KV_HEADS):
        kprep[a] = kc[a]
        vprep[a] = vc[a]
    return jnp.concatenate(outs, axis=1), k, v


def _hgrn_chunk(hq, hf, v, hg, lb, onorm, tri, s_scr):
    c = HG_CHUNK
    q, g, kg = _hgrn_gates(hq, hf, lb)
    g_hi, g_mid, g_lo = _split3(g)
    G = _dot(tri, g_hi) + _dot(tri, g_mid) + _dot(tri, g_lo)
    t_col = lax.broadcasted_iota(jnp.int32, (c, 1), 0)
    ti = lax.broadcasted_iota(jnp.int32, (c, c), 0)
    si = lax.broadcasted_iota(jnp.int32, (c, c), 1)
    levels = []
    half = HG_DIAG
    while half < c:
        blk = 2 * half
        gb = jnp.concatenate(
            [jnp.broadcast_to(G[p * blk + half - 1:p * blk + half, :], (blk, G.shape[1]))
             for p in range(c // blk)], axis=0)
        second = (t_col & half) != 0
        e = jnp.exp(jnp.minimum(jnp.where(second, G - gb, gb - G), 0.0))
        qt = jnp.where(second, q * e, 0.0).astype(BF16)
        kt = jnp.where(second, 0.0, kg * e).astype(BF16)
        shift = blk.bit_length() - 1
        levels.append((qt, kt, (ti >> shift) == (si >> shift)))
        half = blk
    g_last = G[c - 1:c, :]
    q_in = (q * jnp.exp(G)).astype(BF16)
    k_out = (kg * jnp.exp(jnp.minimum(g_last - G, 0.0))).astype(BF16)
    dec_last = jnp.exp(g_last)
    v_bf = v.astype(BF16)
    nblk = c // HG_DIAG
    row_in_blk = lax.broadcasted_iota(jnp.int32, (nblk, HG_DIAG, HG_DV), 1)
    o_heads = []
    for h in range(HG_HEADS):
        hs = slice(h * HG_DK, (h + 1) * HG_DK)
        s_h = s_scr[h]
        a_off = jnp.zeros((c, c), F32)
        for qt, kt, same in levels:
            a_off = a_off + jnp.where(same, _dot_nt(qt[:, hs], kt[:, hs]), 0.0)
        o_h = _dot(q_in[:, hs], s_h.astype(BF16)) + _dot(a_off.astype(BF16), v_bf[:, hs])
        q3 = q[:, hs].reshape(nblk, HG_DIAG, HG_DK)
        k3 = kg[:, hs].reshape(nblk, HG_DIAG, HG_DK)
        g3 = G[:, hs].reshape(nblk, HG_DIAG, HG_DK)
        v3 = v[:, hs].reshape(nblk, HG_DIAG, HG_DV)
        od = jnp.zeros((nblk, HG_DIAG, HG_DV), F32)
        for s in range(HG_DIAG):
            e = jnp.exp(jnp.minimum(g3 - g3[:, s:s + 1, :], 0.0))
            a = jnp.sum(q3 * (k3[:, s:s + 1, :] * e), axis=-1, keepdims=True)
            od = od + a * jnp.where(row_in_blk >= s, v3[:, s:s + 1, :], 0.0)
        o_heads.append(o_h + od.reshape(c, HG_DV))
        s_scr[h] = _col_bcast(dec_last[:, hs]) * s_h + _dot_tn(k_out[:, hs], v_bf[:, hs])
    return _hgrn_out(o_heads, onorm, hg)


def _pool_tile(u, carry, pos0, pw_ref, ps_ref):
    tt = u.shape[0]
    x = jnp.concatenate([carry[...], u], axis=0)
    sums = []
    cur = x
    shift = 1
    for gi in range(len(POOL_WINDOWS)):
        cur = cur + pltpu.roll(cur, shift, 0)
        sums.append(cur[POOL_CARRY:, :POOL_GROUP_DIM])
        cur = cur[:, POOL_GROUP_DIM:]
        shift *= 2
    pos = pos0 + lax.broadcasted_iota(jnp.int32, (tt, 1), 0)
    d_groups = []
    for gi, w in enumerate(POOL_WINDOWS):
        cs = slice(gi * POOL_GROUP_DIM, (gi + 1) * POOL_GROUP_DIM)
        cnt = jnp.minimum(w, pos + 1).astype(F32)
        d_groups.append(sums[gi] / cnt - u[:, cs])
    carry[...] = u[tt - POOL_CARRY:, :]
    return _pool_project(d_groups, pw_ref, ps_ref)


def _layer_mix_kernel(sinks_ref, x_ref, xnext_ref, nw_ref, w_ref, cos_ref, sin_ref, qn_ref, kn_ref, bd_ref,
                      bias_ref, lb_ref, on_ref, tri_ref, pw_ref, ps_ref, wb_ref, wo_ref,
                      x1_ref, kc_ref, vc_ref, s_out_ref, ulast_ref,
                      z_a, z_b, kprep, vprep, s_scr, carry, *, layer, steps_per_seq):
    s = pl.program_id(0)
    n = s % steps_per_seq
    nw = nw_ref[...]

    @pl.when(s == 0)
    def _():
        _project_tile(x_ref[:MIX_TILE, :], nw, w_ref, z_a)

    @pl.when(n == 0)
    def _():
        kprep[...] = jnp.zeros_like(kprep)
        vprep[...] = jnp.zeros_like(vprep)
        s_scr[...] = jnp.zeros_like(s_scr)
        carry[...] = jnp.zeros_like(carry)

    qn = qn_ref[...]
    kn = kn_ref[...]
    bd = bd_ref[...]
    lb = lb_ref[...]
    onorm = on_ref[...]
    tri = tri_ref[...]
    band = bias_ref[1]
    first = bias_ref[jnp.minimum(n, 1)]

    def mix(z_ref, t):
        r0 = t * MIX_TILE
        o_a, o_b = [], []
        for j in range(MIX_TILE // WINDOW):
            rz = slice(j * WINDOW, (j + 1) * WINDOW)
            rx = slice(r0 + j * WINDOW, r0 + (j + 1) * WINDOW)
            ob, k, v = _swa_block(z_ref[rz, 0:P3], cos_ref[rx, :], sin_ref[rx, :], qn, kn, bd,
                                  first if (t == 0 and j == 0) else band, sinks_ref, layer, kprep, vprep)
            o_b.append(ob)
            kc_ref[...] = k
            vc_ref[...] = v
            o_a.append(_hgrn_chunk(z_ref[rz, P3:P4], z_ref[rz, P4:P5], z_ref[rz, P5:P6], z_ref[rz, P6:P7],
                                   lb, onorm, tri, s_scr))
        u = z_ref[:, P7:P8]
        ulast_ref[...] = u[MIX_TILE - POOL_CARRY:, :]
        o_c = _pool_tile(u, carry, n * (2 * MIX_TILE) + r0, pw_ref, ps_ref)
        m = None
        for nb, br in enumerate((jnp.concatenate(o_a, axis=0), jnp.concatenate(o_b, axis=0), o_c)):
            gate = _sigmoid_pair(z_ref[:, P8 + nb * D_MODEL:P8 + (nb + 1) * D_MODEL])[0]
            term = gate * _dot(br.astype(BF16), wb_ref[nb])
            m = term if m is None else m + term
        x1_ref[r0:r0 + MIX_TILE, :] = x_ref[r0:r0 + MIX_TILE, :] + _dot(m.astype(BF16), wo_ref[...])

    _project_tile(x_ref[MIX_TILE:, :], nw, w_ref, z_b)
    mix(z_a, 0)
    _project_tile(xnext_ref[...], nw, w_ref, z_a)
    mix(z_b, 1)
    for h in range(HG_HEADS):
        s_out_ref[h] = s_scr[h]


def _layer_mix(x, norm_w, w_in_bf, cos, sin, qn, kn, bd, bias, sinks, lower, onorm, tri, pool_w_bf, pool_scale,
               w_branch_bf, w_o_bf, layer, batch, seq):
    step_rows = 2 * MIX_TILE
    sps = seq // step_rows
    nsteps = batch * sps
    last_tile = batch * seq // MIX_TILE - 1
    const2 = lambda s: (0, 0)
    lay3 = lambda s: (layer, 0, 0)
    per_seq = lambda s: (s // sps, 0, 0)
    single = pl.Buffered(1)
    return pl.pallas_call(
        functools.partial(_layer_mix_kernel, layer=layer, steps_per_seq=sps),
        grid=(nsteps,),
        in_specs=[
            pl.BlockSpec(memory_space=pltpu.SMEM),
            pl.BlockSpec((step_rows, D_MODEL), lambda s: (s, 0)),
            pl.BlockSpec((MIX_TILE, D_MODEL), lambda s: (jnp.minimum(2 * s + 2, last_tile), 0)),
            pl.BlockSpec((None, 1, D_MODEL), lay3),
            pl.BlockSpec((None, D_MODEL, D_IN), lay3, pipeline_mode=single),
            pl.BlockSpec((step_rows, 128), lambda s: (s % sps, 0)),
            pl.BlockSpec((step_rows, 128), lambda s: (s % sps, 0)),
            pl.BlockSpec((1, SWA_Q), const2),
            pl.BlockSpec((1, SWA_KV), const2),
            pl.BlockSpec((256, 256), const2),
            pl.BlockSpec((2, WINDOW, 2 * WINDOW), lambda s: (0, 0, 0)),
            pl.BlockSpec((None, 1, HG_K), lay3),
            pl.BlockSpec((None, 1, HG_DV), lay3),
            pl.BlockSpec((HG_CHUNK, HG_CHUNK), const2),
            pl.BlockSpec((None, len(POOL_WINDOWS), POOL_GROUP_DIM, POOL_GROUP_DIM), lambda s: (layer, 0, 0, 0)),
            pl.BlockSpec((None, 1, POOL_WIDTH), lay3),
            pl.BlockSpec((None, N_BRANCH, BRANCH_WIDTH, D_MODEL), lambda s: (layer, 0, 0, 0), pipeline_mode=single),
            pl.BlockSpec((None, D_MODEL, D_MODEL), lay3, pipeline_mode=single),
        ],
        out_specs=[
            pl.BlockSpec((step_rows, D_MODEL), lambda s: (s, 0)),
            pl.BlockSpec((None, WINDOW, SWA_KV), per_seq),
            pl.BlockSpec((None, WINDOW, SWA_KV), per_seq),
            pl.BlockSpec((None, HG_HEADS, HG_DK, HG_DV), lambda s: (s // sps, 0, 0, 0)),
            pl.BlockSpec((None, POOL_CARRY, POOL_WIDTH), per_seq),
        ],
        out_shape=[
            jax.ShapeDtypeStruct((batch * seq, D_MODEL), F32),
            jax.ShapeDtypeStruct((batch, WINDOW, SWA_KV), F32),
            jax.ShapeDtypeStruct((batch, WINDOW, SWA_KV), F32),
            jax.ShapeDtypeStruct((batch, HG_HEADS, HG_DK, HG_DV), F32),
            jax.ShapeDtypeStruct((batch, POOL_CARRY, POOL_WIDTH), F32),
        ],
        scratch_shapes=[
            pltpu.VMEM((MIX_TILE, D_IN), F32),
            pltpu.VMEM((MIX_TILE, D_IN), F32),
            pltpu.VMEM((2 * SWA_KV_HEADS, WINDOW, 128), BF16),
            pltpu.VMEM((2 * SWA_KV_HEADS, WINDOW, 128), BF16),
            pltpu.VMEM((HG_HEADS, HG_DK, HG_DV), F32),
            pltpu.VMEM((POOL_CARRY, POOL_WIDTH), F32),
        ],
        compiler_params=_cparams("arbitrary"),
        name="layer_mix",
    )(sinks, x, x, norm_w, w_in_bf, cos, sin, qn, kn, bd, bias, lower, onorm, tri, pool_w_bf, pool_scale,
      w_branch_bf, w_o_bf)


def _ffn_kernel(x_ref, nw_ref, wu_ref, wd_ref, o_ref):
    x = x_ref[...]
    xn = _rms_rows(x, nw_ref[...]).astype(BF16)
    acc = x
    step = D_MODEL
    for c in range(0, D_FF, step):
        h = _dot(xn, wu_ref[:, c:c + step])
        h = jnp.square(jnp.maximum(h, 0.0)).astype(BF16)
        acc = acc + _dot(h, wd_ref[c:c + step, :])
    o_ref[...] = acc


def _ffn(x, norm_w, w_up_bf, w_down_bf, layer, tm):
    m = x.shape[0]
    row = lambda i: (i, 0)
    return pl.pallas_call(
        _ffn_kernel,
        grid=(m // tm,),
        in_specs=[
            pl.BlockSpec((tm, D_MODEL), row),
            pl.BlockSpec((None, 1, D_MODEL), lambda i: (layer, 0, 0)),
            pl.BlockSpec((None, D_MODEL, D_FF), lambda i: (layer, 0, 0), pipeline_mode=pl.Buffered(1)),
            pl.BlockSpec((None, D_FF, D_MODEL), lambda i: (layer, 0, 0), pipeline_mode=pl.Buffered(1)),
        ],
        out_specs=pl.BlockSpec((tm, D_MODEL), row),
        out_shape=jax.ShapeDtypeStruct((m, D_MODEL), F32),
        compiler_params=_cparams("arbitrary"),
        name="ffn",
    )(x, norm_w, w_up_bf, w_down_bf)


def _rope_tables(pos):
    hd = SWA_HEAD_DIM
    inv = jnp.power(ROPE_THETA, -jnp.arange(0, hd, 2, dtype=F32) / hd)
    ang = pos[:, None] * inv[None, :]
    cos = jnp.cos(ang)
    sin = jnp.sin(ang)
    cos_t = jnp.concatenate([cos, cos, cos, cos], axis=1)
    sin_t = jnp.concatenate([-sin, sin, -sin, sin], axis=1)
    return cos_t, sin_t


def _prompt_bias():
    i = jnp.arange(WINDOW)[:, None]
    j = jnp.arange(2 * WINDOW)[None, :]
    band = (j > i) & (j <= i + WINDOW)
    first = band & (j >= WINDOW)
    return jnp.where(jnp.stack([first, band]), 0.0, NEG).astype(F32)


def _sample_bias(wb, t_new):
    grp = 8 // t_new
    n_cache = grp * wb
    nkeys = -(-(n_cache + 8) // 128) * 128
    r = jnp.arange(8)[:, None]
    seq_r, t = r // t_new, r % t_new
    j = jnp.arange(nkeys)[None, :]
    cache_ok = (j < n_cache) & (j // wb == seq_r) & (wb + t - j % wb < WINDOW)
    jn = j - n_cache
    new_ok = (jn >= 0) & (jn < 8) & (jn // t_new == seq_r) & (jn % t_new <= t)
    return jnp.where(cache_ok | new_ok, 0.0, NEG).astype(F32)


def _row_tile(m, cap):
    t = min(m, cap)
    while m % t:
        t //= 2
    return t


def kernel(x_prompt, x_sample, state_hgrn, cache_swa_k, cache_swa_v, state_pool, norm_mix, w_in, q_norm, k_norm,
           attn_sinks, hgrn_lb, hgrn_onorm, pool_w, pool_scale, w_branch, w_o, norm_ffn, w_up, w_down):
    depth = w_in.shape[0]
    bp, seq, _ = x_prompt.shape
    bs, t_new, _ = x_sample.shape
    wb = cache_swa_k.shape[2]

    w_in_bf = w_in.astype(BF16)
    w_branch_bf = w_branch.astype(BF16)
    w_o_bf = w_o.astype(BF16)
    w_up_bf = w_up.astype(BF16)
    w_down_bf = w_down.astype(BF16)
    pool_w_bf = pool_w.astype(BF16)
    norm_mix3 = norm_mix.reshape(depth, 1, D_MODEL)
    norm_ffn3 = norm_ffn.reshape(depth, 1, D_MODEL)
    onorm3 = hgrn_onorm.reshape(depth, 1, HG_DV)
    pool_scale3 = pool_scale.reshape(depth, 1, POOL_WIDTH)
    lower3 = _lower_bounds(hgrn_lb).reshape(depth, 1, HG_K)
    cache_k4 = cache_swa_k.reshape(depth, bs, wb, SWA_KV)
    cache_v4 = cache_swa_v.reshape(depth, bs, wb, SWA_KV)

    cos_p, sin_p = _rope_tables(jnp.arange(seq, dtype=F32))
    cos_s, sin_s = _rope_tables(jnp.arange(t_new, dtype=F32) + PAST_LEN)
    bb_swa = SWA_SAMPLE_BB if bs % SWA_SAMPLE_BB == 0 else bs
    cos_s = jnp.tile(cos_s, (bb_swa, 1))
    sin_s = jnp.tile(sin_s, (bb_swa, 1))
    bias_p = _prompt_bias()
    bias_s = _sample_bias(wb, t_new)
    lane = jnp.arange(256)
    bd = (lane[:, None] // SWA_HEAD_DIM == lane[None, :] // SWA_HEAD_DIM).astype(BF16)
    tok = jnp.arange(HG_CHUNK)
    tri = (tok[:, None] >= tok[None, :]).astype(BF16)

    hp = x_prompt.reshape(bp * seq, D_MODEL)
    hs = x_sample.reshape(bs * t_new, D_MODEL)
    tm_p = _row_tile(bp * seq, 512)
    tm_s = _row_tile(bs * t_new, 512)

    sp = ([], [], [], [])
    ss = ([], [], [], [])
    for l in range(depth):
        qn = jnp.tile(q_norm[l], SWA_HEADS).reshape(1, SWA_Q)
        kn = jnp.tile(k_norm[l], SWA_KV_HEADS).reshape(1, SWA_KV)

        x1, k_last, v_last, s_new, u_last = _layer_mix(
            hp, norm_mix3, w_in_bf, cos_p, sin_p, qn, kn, bd, bias_p, attn_sinks, lower3, onorm3, tri,
            pool_w_bf, pool_scale3, w_branch_bf, w_o_bf, l, bp, seq)
        hp = _ffn(x1, norm_ffn3, w_up_bf, w_down_bf, l, tm_p)
        sp[0].append(s_new)
        sp[1].append(k_last.reshape(bp, WINDOW, SWA_KV_HEADS, SWA_HEAD_DIM))
        sp[2].append(v_last.reshape(bp, WINDOW, SWA_KV_HEADS, SWA_HEAD_DIM))
        sp[3].append(u_last[:, POOL_CARRY - POOL_BUF:])

        qkv, hq, hf, hi, hg, u, zg = _proj_in(hs, norm_mix3, w_in_bf, l, tm_s)
        o_b, k_rot = _swa_sample(qkv, cos_s, sin_s, qn, kn, bd, bias_s, attn_sinks, cache_k4, cache_v4, l, bs, t_new)
        o_a, s_new = _hgrn_sample(hq, hf, hi, hg, lower3, onorm3, state_hgrn, l, bs, t_new)
        u3 = u.reshape(bs, t_new, POOL_WIDTH)
        o_c = _pool_sample(jnp.swapaxes(state_pool[l], 0, 1), jnp.swapaxes(u3, 0, 1),
                           pool_w_bf, pool_scale3, l, t_new)
        o_c = jnp.swapaxes(o_c, 0, 1).reshape(bs * t_new, POOL_WIDTH)
        x1 = _merge(hs, o_a, o_b, o_c, zg, w_branch_bf, w_o_bf, l, tm_s)
        hs = _ffn(x1, norm_ffn3, w_up_bf, w_down_bf, l, tm_s)
        v_new = qkv[:, SWA_Q + SWA_KV:].reshape(bs, t_new, SWA_KV)
        kk = jnp.concatenate([cache_k4[l], k_rot.reshape(bs, t_new, SWA_KV)], axis=1)[:, -wb:]
        vv = jnp.concatenate([cache_v4[l], v_new], axis=1)[:, -wb:]
        ss[0].append(s_new)
        ss[1].append(kk.reshape(bs, wb, SWA_KV_HEADS, SWA_HEAD_DIM))
        ss[2].append(vv.reshape(bs, wb, SWA_KV_HEADS, SWA_HEAD_DIM))
        ss[3].append(jnp.concatenate([state_pool[l], u3], axis=1)[:, -POOL_BUF:])

    return (hp.reshape(bp, seq, D_MODEL), hs.reshape(bs, t_new, D_MODEL),
            jnp.stack(sp[0]), jnp.stack(sp[1]), jnp.stack(sp[2]), jnp.stack(sp[3]),
            jnp.stack(ss[0]), jnp.stack(ss[1]), jnp.stack(ss[2]), jnp.stack(ss[3]))
```

```python
import functools

import jax
import jax.numpy as jnp
from jax import lax
from jax.experimental import pallas as pl
from jax.experimental.pallas import tpu as pltpu

F32 = jnp.float32
BF16 = jnp.bfloat16

D_MODEL = 1024
PAST_LEN = 8192
SWA_HEADS = 8
SWA_KV_HEADS = 2
SWA_HEAD_DIM = 64
SWA_GROUP = SWA_HEADS // SWA_KV_HEADS
WINDOW = 128
ROPE_THETA = 10000.0
HG_HEADS = 4
HG_DK = 128
HG_DV = 128
LB_FLOOR = 1e-30
POOL_WINDOWS = (2, 4, 8, 16)
POOL_GROUP_DIM = 128
POOL_WIDTH = 512
POOL_BUF = 15
BRANCH_WIDTH = 512
N_BRANCH = 3
D_FF = 4 * D_MODEL
EPS = 1e-6
NEG = -1e30

SWA_Q = SWA_HEADS * SWA_HEAD_DIM
SWA_KV = SWA_KV_HEADS * SWA_HEAD_DIM
HG_K = HG_HEADS * HG_DK
QKV_W = SWA_Q + 2 * SWA_KV
P3 = QKV_W
P4 = P3 + HG_K
P5 = P4 + HG_K
P6 = P5 + HG_K
P7 = P6 + HG_K
P8 = P7 + POOL_WIDTH
D_IN = P8 + N_BRANCH * D_MODEL

VMEM_LIMIT_BYTES = 56 * 1024 * 1024
HG_CHUNK = 128
HG_DIAG = 8
POOL_TILE = 256
POOL_CARRY = 16
SAMPLE_BB = 8
SWA_BPS = 2
MIX_TILE = 256
SWA_SAMPLE_BB = 16


def _cparams(*sem):
    return pltpu.CompilerParams(dimension_semantics=sem, vmem_limit_bytes=VMEM_LIMIT_BYTES)


def _rms_rows(x, w):
    ms = jnp.mean(x * x, axis=-1, keepdims=True)
    return x * lax.rsqrt(ms + EPS) * w


def _sigmoid(z):
    return 1.0 / (1.0 + jnp.exp(-z))


def _silu(z):
    return z * _sigmoid(z)


def _split3(x):
    hi = x.astype(BF16)
    r1 = x - hi.astype(F32)
    mid = r1.astype(BF16)
    lo = (r1 - mid.astype(F32)).astype(BF16)
    return hi, mid, lo


def _dot(a, b):
    return jnp.dot(a, b, preferred_element_type=F32)


def _dot_nt(a, b):
    return lax.dot_general(a, b, (((1,), (1,)), ((), ())), preferred_element_type=F32)


def _dot_tn(a, b):
    return lax.dot_general(a, b, (((0,), (0,)), ((), ())), preferred_element_type=F32)


def _lower_kernel(lb_ref, o_ref):
    x = lb_ref[...]
    depth = x.shape[0]
    rows = [x[i:i + 1] for i in range(depth)]
    m = rows[0]
    for r in rows[1:]:
        m = jnp.maximum(m, r)
    es = [jnp.exp(r - m) for r in rows]
    tot = es[0]
    for e in es[1:]:
        tot = tot + e
    ps = [e / tot for e in es]
    out = []
    c = None
    for p in ps:
        c = p if c is None else c + p
        out.append(jnp.maximum(c - ps[0], 0.0))
    o_ref[...] = jnp.concatenate(out, axis=0)


def _lower_bounds(hgrn_lb):
    return pl.pallas_call(
        _lower_kernel,
        out_shape=jax.ShapeDtypeStruct(hgrn_lb.shape, F32),
        name="lower_bounds",
    )(hgrn_lb)


def _proj_in_kernel(x_ref, nw_ref, w_ref, qkv_ref, hq_ref, hf_ref, hi_ref, hg_ref, u_ref, zg_ref):
    xn = _rms_rows(x_ref[...], nw_ref[...]).astype(BF16)

    def mm(a, b):
        return _dot(xn, w_ref[:, a:b])

    qkv_ref[...] = mm(0, P3)
    hq_ref[...] = mm(P3, P4)
    hf_ref[...] = mm(P4, P5)
    hi_ref[...] = mm(P5, P6)
    hg_ref[...] = mm(P6, P7)
    u_ref[...] = mm(P7, P8)
    for n in range(N_BRANCH):
        zg_ref[:, n * D_MODEL:(n + 1) * D_MODEL] = mm(P8 + n * D_MODEL, P8 + (n + 1) * D_MODEL)


def _proj_in(x, norm_w, w_in_bf, layer, tm):
    m = x.shape[0]
    widths = (QKV_W, HG_K, HG_K, HG_K, HG_K, POOL_WIDTH, N_BRANCH * D_MODEL)
    return pl.pallas_call(
        _proj_in_kernel,
        grid=(m // tm,),
        in_specs=[
            pl.BlockSpec((tm, D_MODEL), lambda i: (i, 0)),
            pl.BlockSpec((None, 1, D_MODEL), lambda i: (layer, 0, 0)),
            pl.BlockSpec((None, D_MODEL, D_IN), lambda i: (layer, 0, 0), pipeline_mode=pl.Buffered(1)),
        ],
        out_specs=[pl.BlockSpec((tm, w), lambda i: (i, 0)) for w in widths],
        out_shape=[jax.ShapeDtypeStruct((m, w), F32) for w in widths],
        compiler_params=_cparams("arbitrary"),
        name="proj_in",
    )(x, norm_w, w_in_bf)


def _head_norm_rope(x, w_row, cos, sin, bd):
    width = x.shape[1]
    sq = x * x
    hi = sq.astype(BF16)
    lo = (sq - hi.astype(F32)).astype(BF16)
    parts = []
    for c in range(0, width, 256):
        wc = min(256, width - c)
        b = bd[:wc, :wc]
        parts.append(_dot(hi[:, c:c + wc], b) + _dot(lo[:, c:c + wc], b))
    ms = (parts[0] if len(parts) == 1 else jnp.concatenate(parts, axis=1)) * (1.0 / SWA_HEAD_DIM)
    y = x * lax.rsqrt(ms + EPS) * w_row
    reps = width // 128
    cos_t = cos if reps == 1 else jnp.concatenate([cos] * reps, axis=1)
    sin_t = sin if reps == 1 else jnp.concatenate([sin] * reps, axis=1)
    lane = lax.broadcasted_iota(jnp.int32, y.shape, 1)
    first_half = (lane & (SWA_HEAD_DIM - 1)) < (SWA_HEAD_DIM // 2)
    half = SWA_HEAD_DIM // 2
    rot = jnp.where(first_half, pltpu.roll(y, width - half, 1), pltpu.roll(y, half, 1))
    return y * cos_t + rot * sin_t


def _kv_pad(x):
    lo = lax.broadcasted_iota(jnp.int32, x.shape, 1) < SWA_HEAD_DIM
    xr = pltpu.roll(x, SWA_HEAD_DIM, 1)
    return (jnp.where(lo, x, 0.0), jnp.where(lo, 0.0, xr), jnp.where(lo, xr, 0.0), jnp.where(lo, 0.0, x))


def _sink_softmax_rows(sm, sink):
    mx = jnp.maximum(jnp.max(sm, axis=-1, keepdims=True), sink)
    e = jnp.exp(sm - mx)
    den = jnp.sum(e, axis=-1, keepdims=True) + jnp.exp(sink - mx)
    return e * (1.0 / den)


def _swa_prompt_kernel(sinks_ref, qkv_ref, cos_ref, sin_ref, qn_ref, kn_ref, bd_ref, bias_ref,
                       o_ref, kc_ref, kprep, vprep, *, layer):
    n = pl.program_id(1)

    @pl.when(n == 0)
    def _():
        kprep[...] = jnp.zeros_like(kprep)
        vprep[...] = jnp.zeros_like(vprep)

    cos = cos_ref[...]
    sin = sin_ref[...]
    bd = bd_ref[...]
    bias = bias_ref[...]
    scale = SWA_HEAD_DIM ** -0.5
    for i in range(qkv_ref.shape[0]):
        z = qkv_ref[i]
        q = _head_norm_rope(z[:, :SWA_Q], qn_ref[...], cos, sin, bd) * scale
        k = _head_norm_rope(z[:, SWA_Q:SWA_Q + SWA_KV], kn_ref[...], cos, sin, bd)
        v = z[:, SWA_Q + SWA_KV:]
        kc = [a.astype(BF16) for a in _kv_pad(k)]
        vc = [a.astype(BF16) for a in _kv_pad(v)]
        outs = []
        for kvh in range(SWA_KV_HEADS):
            a, b = 2 * kvh, 2 * kvh + 1
            keys = jnp.concatenate([kprep[i, a], kc[a], kprep[i, b], kc[b]], axis=0)
            vals = jnp.concatenate([vprep[i, a], vc[a], vprep[i, b], vc[b]], axis=0)
            qq = jnp.concatenate([q[:, a * 128:(a + 1) * 128], q[:, b * 128:(b + 1) * 128]], axis=0)
            s = _dot_nt(qq.astype(BF16), keys)
            ps = []
            for half in range(2):
                quads = []
                for r in range(2):
                    sm = s[r * WINDOW:(r + 1) * WINDOW, half * 2 * WINDOW:(half + 1) * 2 * WINDOW] + bias
                    sink = sinks_ref[layer, SWA_GROUP * kvh + 2 * r + half]
                    quads.append(_sink_softmax_rows(sm, sink).astype(BF16))
                ps.append(jnp.concatenate(quads, axis=0))
            og = _dot(jnp.concatenate(ps, axis=1), vals)
            outs.extend([og[:WINDOW], og[WINDOW:]])
        o_ref[i] = jnp.concatenate(outs, axis=1)
        kc_ref[i] = k
        for a in range(2 * SWA_KV_HEADS):
            kprep[i, a] = kc[a]
            vprep[i, a] = vc[a]


def _swa_prompt(qkv, cos, sin, qn, kn, bd, bias, sinks, layer, batch, seq):
    nb = seq // WINDOW
    bps = SWA_BPS if batch % SWA_BPS == 0 else 1
    const = lambda b, n: (0, 0)
    return pl.pallas_call(
        functools.partial(_swa_prompt_kernel, layer=layer),
        grid=(batch // bps, nb),
        in_specs=[
            pl.BlockSpec(memory_space=pltpu.SMEM),
            pl.BlockSpec((bps, WINDOW, QKV_W), lambda b, n: (b, n, 0)),
            pl.BlockSpec((WINDOW, 128), lambda b, n: (n, 0)),
            pl.BlockSpec((WINDOW, 128), lambda b, n: (n, 0)),
            pl.BlockSpec((1, SWA_Q), const),
            pl.BlockSpec((1, SWA_KV), const),
            pl.BlockSpec((256, 256), const),
            pl.BlockSpec((None, WINDOW, 2 * WINDOW), lambda b, n: (jnp.minimum(n, 1), 0, 0)),
        ],
        out_specs=[
            pl.BlockSpec((bps, WINDOW, SWA_Q), lambda b, n: (b, n, 0)),
            pl.BlockSpec((bps, WINDOW, SWA_KV), lambda b, n: (b, 0, 0)),
        ],
        out_shape=[
            jax.ShapeDtypeStruct((batch, seq, SWA_Q), F32),
            jax.ShapeDtypeStruct((batch, WINDOW, SWA_KV), F32),
        ],
        scratch_shapes=[pltpu.VMEM((bps, 2 * SWA_KV_HEADS, WINDOW, 128), BF16),
                        pltpu.VMEM((bps, 2 * SWA_KV_HEADS, WINDOW, 128), BF16)],
        compiler_params=_cparams("arbitrary", "arbitrary"),
        name="swa_prompt",
    )(sinks, qkv.reshape(batch, seq, QKV_W), cos, sin, qn, kn, bd, bias)


def _swa_sample_kernel(sinks_ref, qkv_ref, cos_ref, sin_ref, qn_ref, kn_ref, bd_ref, bias_ref, kb_ref, vb_ref,
                       *rest, layer, t_new):
    o_ref, kc_out_ref, vc_out_ref = rest[-3:]
    z = qkv_ref[...]
    cos = cos_ref[...]
    sin = sin_ref[...]
    bd = bd_ref[...]
    q = _head_norm_rope(z[:, :SWA_Q], qn_ref[...], cos, sin, bd) * (SWA_HEAD_DIM ** -0.5)
    k = _head_norm_rope(z[:, SWA_Q:SWA_Q + SWA_KV], kn_ref[...], cos, sin, bd)
    v = z[:, SWA_Q + SWA_KV:]

    rows = z.shape[0]
    ng = rows // 8
    nkeys = bias_ref.shape[1]
    nseq, wb = kb_ref.shape[0], kb_ref.shape[1]
    for cache_ref, out_ref, new in ((kb_ref, kc_out_ref, k), (vb_ref, vc_out_ref, v)):
        flat = cache_ref[...].reshape(nseq * wb, SWA_KV)
        out_ref[...] = pltpu.roll(flat, nseq * wb - t_new, 0).reshape(nseq, wb, SWA_KV)
        for b in range(nseq):
            out_ref[b, wb - t_new:wb, :] = new[b * t_new:(b + 1) * t_new]
    lo = lax.broadcasted_iota(jnp.int32, (rows, 128), 1) < SWA_HEAD_DIM
    qw = []
    for h in range(SWA_HEADS):
        qc = q[:, (h // 2) * 128:(h // 2 + 1) * 128]
        want_hi = h // SWA_GROUP == 1
        if (h % 2 == 1) != want_hi:
            qc = pltpu.roll(qc, SWA_HEAD_DIM, 1)
        qw.append((jnp.where(lo, 0.0, qc) if want_hi else jnp.where(lo, qc, 0.0)).reshape(ng, 8, 128))
    qw = jnp.concatenate(qw, axis=1).astype(BF16)

    def keys_of(cache_ref, new):
        cache = cache_ref[...].reshape(ng, (8 // t_new) * wb, SWA_KV)
        pad = jnp.zeros((ng, nkeys - cache.shape[1] - 8, SWA_KV), F32)
        return jnp.concatenate([cache, new.reshape(ng, 8, SWA_KV), pad], axis=1).astype(BF16)

    s = jnp.einsum('gqd,gkd->gqk', qw, keys_of(kb_ref, k), preferred_element_type=F32)
    bias = bias_ref[...]
    p = jnp.concatenate(
        [_sink_softmax_rows(s[:, h * 8:(h + 1) * 8, :] + bias, sinks_ref[layer, h]).astype(BF16)
         for h in range(SWA_HEADS)], axis=1)
    res = jnp.einsum('gqk,gkd->gqd', p, keys_of(vb_ref, v), preferred_element_type=F32)
    res = res.reshape(ng * SWA_HEADS * 8, 128)
    res_r = pltpu.roll(res, SWA_HEAD_DIM, 1)
    res = res.reshape(ng, SWA_HEADS * 8, 128)
    res_r = res_r.reshape(ng, SWA_HEADS * 8, 128)
    lo3 = lo.reshape(ng, 8, 128)
    cols = []
    for c in range(SWA_HEADS // 2):
        kv_hi = (2 * c) // SWA_GROUP == 1
        even = (res_r if kv_hi else res)[:, (2 * c) * 8:(2 * c + 1) * 8, :]
        odd = (res if kv_hi else res_r)[:, (2 * c + 1) * 8:(2 * c + 2) * 8, :]
        cols.append(jnp.where(lo3, even, odd).reshape(rows, 128))
    o_ref[...] = jnp.concatenate(cols, axis=1)


def _swa_sample(qkv, cos, sin, qn, kn, bd, bias, sinks, kbuf, vbuf, stacked_prev, layer, batch, t_new):
    bb = SWA_SAMPLE_BB if batch % SWA_SAMPLE_BB == 0 else batch
    rb = bb * t_new
    wb = kbuf.shape[2]
    row = lambda i: (i, 0)
    const = lambda i: (0, 0)
    cache_blk = pl.BlockSpec((None, bb, wb, SWA_KV), lambda i: (layer, i, 0, 0))
    any_spec = pl.BlockSpec(memory_space=pl.ANY)
    return pl.pallas_call(
        functools.partial(_swa_sample_kernel, layer=layer, t_new=t_new),
        grid=(batch // bb,),
        in_specs=[
            pl.BlockSpec(memory_space=pltpu.SMEM),
            pl.BlockSpec((rb, QKV_W), row),
            pl.BlockSpec((rb, 128), const),
            pl.BlockSpec((rb, 128), const),
            pl.BlockSpec((1, SWA_Q), const),
            pl.BlockSpec((1, SWA_KV), const),
            pl.BlockSpec((256, 256), const),
            pl.BlockSpec(bias.shape, const),
            cache_blk,
            cache_blk,
            any_spec,
            any_spec,
        ],
        out_specs=[pl.BlockSpec((rb, SWA_Q), row), cache_blk, cache_blk],
        out_shape=[
            jax.ShapeDtypeStruct((batch * t_new, SWA_Q), F32),
            jax.ShapeDtypeStruct(kbuf.shape, F32),
            jax.ShapeDtypeStruct(vbuf.shape, F32),
        ],
        input_output_aliases={10: 1, 11: 2},
        compiler_params=_cparams("arbitrary"),
        name="swa_sample",
    )(sinks, qkv, cos, sin, qn, kn, bd, bias, kbuf, vbuf, stacked_prev[0], stacked_prev[1])


def _hgrn_gates(hq, hf, lb):
    sig_p = _sigmoid(hf)
    om = 1.0 - lb
    logf = jnp.log(jnp.maximum(lb, LB_FLOOR) + om * sig_p)
    return _silu(hq), logf, om * (1.0 - sig_p)


def _hgrn_out(o, onorm, hg):
    outs = []
    for h in range(HG_HEADS):
        hs = slice(h * HG_DV, (h + 1) * HG_DV)
        outs.append(_rms_rows(o[h], onorm) * _silu(hg[:, hs]))
    return jnp.concatenate(outs, axis=1)


def _col_bcast(row):
    return jnp.transpose(jnp.broadcast_to(row, (row.shape[1], row.shape[1])))


def _hgrn_prompt_kernel(hq_ref, hf_ref, hi_ref, hg_ref, lb_ref, on_ref, tri_ref,
                        o_ref, s_out_ref, s_scr):
    n = pl.program_id(1)

    @pl.when(n == 0)
    def _():
        s_scr[...] = jnp.zeros_like(s_scr)

    c = HG_CHUNK
    q, g, kg = _hgrn_gates(hq_ref[...], hf_ref[...], lb_ref[...])
    v = hi_ref[...]
    tri = tri_ref[...]
    g_hi, g_mid, g_lo = _split3(g)
    G = _dot(tri, g_hi) + _dot(tri, g_mid) + _dot(tri, g_lo)

    t_col = lax.broadcasted_iota(jnp.int32, (c, 1), 0)
    ti = lax.broadcasted_iota(jnp.int32, (c, c), 0)
    si = lax.broadcasted_iota(jnp.int32, (c, c), 1)

    levels = []
    half = HG_DIAG
    while half < c:
        blk = 2 * half
        gb = jnp.concatenate(
            [jnp.broadcast_to(G[p * blk + half - 1:p * blk + half, :], (blk, G.shape[1]))
             for p in range(c // blk)], axis=0)
        second = (t_col & half) != 0
        e = jnp.exp(jnp.minimum(jnp.where(second, G - gb, gb - G), 0.0))
        qt = jnp.where(second, q * e, 0.0).astype(BF16)
        kt = jnp.where(second, 0.0, kg * e).astype(BF16)
        shift = blk.bit_length() - 1
        levels.append((qt, kt, (ti >> shift) == (si >> shift)))
        half = blk

    g_last = G[c - 1:c, :]
    q_in = (q * jnp.exp(G)).astype(BF16)
    k_out = (kg * jnp.exp(jnp.minimum(g_last - G, 0.0))).astype(BF16)
    dec_last = jnp.exp(g_last)
    v_bf = v.astype(BF16)

    nblk = c // HG_DIAG
    row_in_blk = lax.broadcasted_iota(jnp.int32, (nblk, HG_DIAG, HG_DV), 1)
    o_heads = []
    for h in range(HG_HEADS):
        hs = slice(h * HG_DK, (h + 1) * HG_DK)
        s_h = s_scr[h]
        a_off = jnp.zeros((c, c), F32)
        for qt, kt, same in levels:
            a_off = a_off + jnp.where(same, _dot_nt(qt[:, hs], kt[:, hs]), 0.0)
        o_h = _dot(q_in[:, hs], s_h.astype(BF16)) + _dot(a_off.astype(BF16), v_bf[:, hs])
        q3 = q[:, hs].reshape(nblk, HG_DIAG, HG_DK)
        k3 = kg[:, hs].reshape(nblk, HG_DIAG, HG_DK)
        g3 = G[:, hs].reshape(nblk, HG_DIAG, HG_DK)
        v3 = v[:, hs].reshape(nblk, HG_DIAG, HG_DV)
        od = jnp.zeros((nblk, HG_DIAG, HG_DV), F32)
        for s in range(HG_DIAG):
            e = jnp.exp(jnp.minimum(g3 - g3[:, s:s + 1, :], 0.0))
            a = jnp.sum(q3 * (k3[:, s:s + 1, :] * e), axis=-1, keepdims=True)
            od = od + a * jnp.where(row_in_blk >= s, v3[:, s:s + 1, :], 0.0)
        o_heads.append(o_h + od.reshape(c, HG_DV))
        s_new = _col_bcast(dec_last[:, hs]) * s_h + _dot_tn(k_out[:, hs], v_bf[:, hs])
        s_scr[h] = s_new
        s_out_ref[h] = s_new
    o_ref[...] = _hgrn_out(o_heads, on_ref[...], hg_ref[...])


def _hgrn_prompt(hq, hf, hi, hg, lower, onorm, tri, layer, batch, seq):
    nc = seq // HG_CHUNK
    row = lambda b, n: (b * nc + n, 0)
    blk = pl.BlockSpec((HG_CHUNK, HG_K), row)
    return pl.pallas_call(
        _hgrn_prompt_kernel,
        grid=(batch, nc),
        in_specs=[
            blk, blk, blk, blk,
            pl.BlockSpec((None, 1, HG_K), lambda b, n: (layer, 0, 0)),
            pl.BlockSpec((None, 1, HG_DV), lambda b, n: (layer, 0, 0)),
            pl.BlockSpec((HG_CHUNK, HG_CHUNK), lambda b, n: (0, 0)),
        ],
        out_specs=[blk, pl.BlockSpec((None, HG_HEADS, HG_DK, HG_DV), lambda b, n: (b, 0, 0, 0))],
        out_shape=[
            jax.ShapeDtypeStruct((batch * seq, HG_K), F32),
            jax.ShapeDtypeStruct((batch, HG_HEADS, HG_DK, HG_DV), F32),
        ],
        scratch_shapes=[pltpu.VMEM((HG_HEADS, HG_DK, HG_DV), F32)],
        compiler_params=_cparams("arbitrary", "arbitrary"),
        name="hgrn_prompt",
    )(hq, hf, hi, hg, lower, onorm, tri)


def _hgrn_sample_kernel(hq_ref, hf_ref, hi_ref, hg_ref, lb_ref, on_ref, s0_ref, *rest, t_new):
    o_ref, s_out_ref = rest[-2:]
    q, g, kg = _hgrn_gates(hq_ref[...], hf_ref[...], lb_ref[...])
    v = hi_ref[...]
    rows = q.shape[0]
    t_idx = lax.broadcasted_iota(jnp.int32, (rows, 1), 0) % t_new

    def down(x, d):
        return pltpu.roll(x, d, 0)

    def up(x, d):
        return pltpu.roll(x, rows - d, 0)

    G = g
    for d in range(1, t_new):
        G = G + jnp.where(t_idx >= d, down(g, d), 0.0)
    rest = jnp.zeros_like(g)
    for d in range(1, t_new):
        rest = rest + jnp.where(t_idx + d < t_new, up(g, d), 0.0)
    q_in = q * jnp.exp(G)
    k_out = kg * jnp.exp(rest)
    dec_all = jnp.exp(G + rest)

    o_intra = [jnp.zeros((rows, HG_DV), F32) for _ in range(HG_HEADS)]
    for d in range(t_new):
        kd = kg if d == 0 else down(kg, d)
        vd = v if d == 0 else down(v, d)
        gd = G if d == 0 else down(G, d)
        w = q * kd * jnp.exp(jnp.minimum(G - gd, 0.0))
        for h in range(HG_HEADS):
            hs = slice(h * HG_DK, (h + 1) * HG_DK)
            a = jnp.sum(w[:, hs], axis=-1, keepdims=True)
            o_intra[h] = o_intra[h] + jnp.where(t_idx >= d, a, 0.0) * vd[:, hs]

    dec_hi = dec_all.astype(BF16).astype(F32)
    dec_r = dec_all - dec_hi
    dec_mid = dec_r.astype(BF16).astype(F32)
    dec_lo = (dec_r - dec_mid).astype(BF16).astype(F32)
    dec_parts = jnp.where(t_idx == 0, dec_hi, jnp.where(t_idx == 1, dec_mid, jnp.where(t_idx == 2, dec_lo, 0.0)))
    seq_of_row = lax.broadcasted_iota(jnp.int32, (rows, 1), 0) // t_new
    ones = jnp.ones((rows, HG_DV), BF16)
    v_bf = v.astype(BF16)
    o = []
    for h in range(HG_HEADS):
        hs = slice(h * HG_DK, (h + 1) * HG_DK)
        o_h = o_intra[h]
        for b in range(s0_ref.shape[0]):
            mine = seq_of_row == b
            s_h = s0_ref[b, h]
            o_h = o_h + _dot(jnp.where(mine, q_in[:, hs], 0.0).astype(BF16), s_h.astype(BF16))
            dec_mat = _dot_tn(jnp.where(mine, dec_parts[:, hs], 0.0).astype(BF16), ones)
            upd = _dot_tn(jnp.where(mine, k_out[:, hs], 0.0).astype(BF16), v_bf[:, hs])
            s_out_ref[b, h] = dec_mat * s_h + upd
        o.append(o_h)
    o_ref[...] = _hgrn_out(o, on_ref[...], hg_ref[...])


def _hgrn_sample(hq, hf, hi, hg, lower, onorm, state, layer, batch, t_new):
    assert t_new >= 3
    rb = SAMPLE_BB * t_new
    blk = pl.BlockSpec((rb, HG_K), lambda i: (i, 0))
    return pl.pallas_call(
        functools.partial(_hgrn_sample_kernel, t_new=t_new),
        grid=(batch // SAMPLE_BB,),
        in_specs=[
            blk, blk, blk, blk,
            pl.BlockSpec((None, 1, HG_K), lambda i: (layer, 0, 0)),
            pl.BlockSpec((None, 1, HG_DV), lambda i: (layer, 0, 0)),
            pl.BlockSpec((None, SAMPLE_BB, HG_HEADS, HG_DK, HG_DV), lambda i: (layer, i, 0, 0, 0)),
        ],
        out_specs=[blk, pl.BlockSpec((SAMPLE_BB, HG_HEADS, HG_DK, HG_DV), lambda i: (i, 0, 0, 0))],
        out_shape=[
            jax.ShapeDtypeStruct((batch * t_new, HG_K), F32),
            jax.ShapeDtypeStruct((batch, HG_HEADS, HG_DK, HG_DV), state.dtype),
        ],
        compiler_params=_cparams("arbitrary"),
        name="hgrn_sample",
    )(hq, hf, hi, hg, lower, onorm, state)


def _pool_project(d_groups, pw_ref, ps_ref):
    outs = []
    for gi, d in enumerate(d_groups):
        cs = slice(gi * POOL_GROUP_DIM, (gi + 1) * POOL_GROUP_DIM)
        outs.append(_dot(d.astype(BF16), pw_ref[gi]) * ps_ref[:, cs])
    return jnp.concatenate(outs, axis=1)


def _pool_prompt_kernel(u_ref, pw_ref, ps_ref, o_ref, carry):
    n = pl.program_id(1)

    @pl.when(n == 0)
    def _():
        carry[...] = jnp.zeros_like(carry)

    tt = u_ref.shape[0]
    u = u_ref[...]
    x = jnp.concatenate([carry[...], u], axis=0)
    sums = []
    cur = x
    shift = 1
    for gi in range(len(POOL_WINDOWS)):
        cur = cur + pltpu.roll(cur, shift, 0)
        sums.append(cur[POOL_CARRY:, :POOL_GROUP_DIM])
        cur = cur[:, POOL_GROUP_DIM:]
        shift *= 2
    pos = n * tt + lax.broadcasted_iota(jnp.int32, (tt, 1), 0)
    d_groups = []
    for gi, w in enumerate(POOL_WINDOWS):
        cs = slice(gi * POOL_GROUP_DIM, (gi + 1) * POOL_GROUP_DIM)
        cnt = jnp.minimum(w, pos + 1).astype(F32)
        d_groups.append(sums[gi] / cnt - u[:, cs])
    o_ref[...] = _pool_project(d_groups, pw_ref, ps_ref)
    carry[...] = u[tt - POOL_CARRY:, :]


def _pool_prompt(u, pool_w_bf, pool_scale, layer, batch, seq):
    nt = seq // POOL_TILE
    row = lambda b, n: (b * nt + n, 0)
    return pl.pallas_call(
        _pool_prompt_kernel,
        grid=(batch, nt),
        in_specs=[
            pl.BlockSpec((POOL_TILE, POOL_WIDTH), row),
            pl.BlockSpec((None, len(POOL_WINDOWS), POOL_GROUP_DIM, POOL_GROUP_DIM), lambda b, n: (layer, 0, 0, 0)),
            pl.BlockSpec((None, 1, POOL_WIDTH), lambda b, n: (layer, 0, 0)),
        ],
        out_specs=pl.BlockSpec((POOL_TILE, POOL_WIDTH), row),
        out_shape=jax.ShapeDtypeStruct((batch * seq, POOL_WIDTH), F32),
        scratch_shapes=[pltpu.VMEM((POOL_CARRY, POOL_WIDTH), F32)],
        compiler_params=_cparams("arbitrary", "arbitrary"),
        name="pool_prompt",
    )(u, pool_w_bf, pool_scale)


def _pool_sample_kernel(pb_ref, u_ref, pw_ref, ps_ref, o_ref, *, t_new):
    def row(i):
        return pb_ref[i] if i < POOL_BUF else u_ref[i - POOL_BUF]

    outs = []
    for t in range(t_new):
        e = POOL_BUF + t
        d_groups = []
        for gi, w in enumerate(POOL_WINDOWS):
            cs = slice(gi * POOL_GROUP_DIM, (gi + 1) * POOL_GROUP_DIM)
            lo = max(e + 1 - w, 0)
            acc = row(lo)[:, cs]
            for i in range(lo + 1, e + 1):
                acc = acc + row(i)[:, cs]
            d_groups.append(acc / float(e + 1 - lo) - u_ref[t][:, cs])
        outs.append(_pool_project(d_groups, pw_ref, ps_ref))
    for t in range(t_new):
        o_ref[t] = outs[t]


def _pool_sample(pbuf_t, u_t, pool_w_bf, pool_scale, layer, t_new):
    batch = u_t.shape[1]
    return pl.pallas_call(
        functools.partial(_pool_sample_kernel, t_new=t_new),
        grid=(1,),
        in_specs=[
            pl.BlockSpec((POOL_BUF, batch, POOL_WIDTH), lambda i: (0, 0, 0)),
            pl.BlockSpec((t_new, batch, POOL_WIDTH), lambda i: (0, 0, 0)),
            pl.BlockSpec((None, len(POOL_WINDOWS), POOL_GROUP_DIM, POOL_GROUP_DIM), lambda i: (layer, 0, 0, 0)),
            pl.BlockSpec((None, 1, POOL_WIDTH), lambda i: (layer, 0, 0)),
        ],
        out_specs=pl.BlockSpec((t_new, batch, POOL_WIDTH), lambda i: (0, 0, 0)),
        out_shape=jax.ShapeDtypeStruct((t_new, batch, POOL_WIDTH), F32),
        compiler_params=_cparams("arbitrary"),
        name="pool_sample",
    )(pbuf_t, u_t, pool_w_bf, pool_scale)


def _merge_kernel(x_ref, oa_ref, ob_ref, oc_ref, zg_ref, wb_ref, wo_ref, o_ref):
    m = None
    for n, br in enumerate((oa_ref, ob_ref, oc_ref)):
        proj = _dot(br[...].astype(BF16), wb_ref[n])
        gate = _sigmoid(zg_ref[:, n * D_MODEL:(n + 1) * D_MODEL])
        m = gate * proj if m is None else m + gate * proj
    o_ref[...] = x_ref[...] + _dot(m.astype(BF16), wo_ref[...])


def _merge(x, oa, ob, oc, zg, w_branch_bf, w_o_bf, layer, tm):
    m = x.shape[0]
    row = lambda i: (i, 0)
    return pl.pallas_call(
        _merge_kernel,
        grid=(m // tm,),
        in_specs=[
            pl.BlockSpec((tm, D_MODEL), row),
            pl.BlockSpec((tm, BRANCH_WIDTH), row),
            pl.BlockSpec((tm, BRANCH_WIDTH), row),
            pl.BlockSpec((tm, BRANCH_WIDTH), row),
            pl.BlockSpec((tm, N_BRANCH * D_MODEL), row),
            pl.BlockSpec((None, N_BRANCH, BRANCH_WIDTH, D_MODEL), lambda i: (layer, 0, 0, 0)),
            pl.BlockSpec((None, D_MODEL, D_MODEL), lambda i: (layer, 0, 0)),
        ],
        out_specs=pl.BlockSpec((tm, D_MODEL), row),
        out_shape=jax.ShapeDtypeStruct((m, D_MODEL), F32),
        compiler_params=_cparams("arbitrary"),
        name="merge",
    )(x, oa, ob, oc, zg, w_branch_bf, w_o_bf)


_Z_GROUPS = ((0, P3), (P3, P7), (P7, P8)) + tuple(
    (P8 + n * D_MODEL, P8 + (n + 1) * D_MODEL) for n in range(N_BRANCH))


def _project_tile(x, nw, w_ref, z_ref):
    xn = _rms_rows(x, nw).astype(BF16)
    for a, b in _Z_GROUPS:
        z_ref[:, a:b] = _dot(xn, w_ref[:, a:b])


def _swa_block(z, cos, sin, qn, kn, bd, bias, sinks_ref, layer, kprep, vprep):
    q = _head_norm_rope(z[:, :SWA_Q], qn, cos, sin, bd) * (SWA_HEAD_DIM ** -0.5)
    k = _head_norm_rope(z[:, SWA_Q:SWA_Q + SWA_KV], kn, cos, sin, bd)
    v = z[:, SWA_Q + SWA_KV:]
    kc = [a.astype(BF16) for a in _kv_pad(k)]
    vc = [a.astype(BF16) for a in _kv_pad(v)]
    outs = []
    for kvh in range(SWA_KV_HEADS):
        a, b = 2 * kvh, 2 * kvh + 1
        keys = jnp.concatenate([kprep[a], kc[a], kprep[b], kc[b]], axis=0)
        vals = jnp.concatenate([vprep[a], vc[a], vprep[b], vc[b]], axis=0)
        qq = jnp.concatenate([q[:, a * 128:(a + 1) * 128], q[:, b * 128:(b + 1) * 128]], axis=0)
        s = _dot_nt(qq.astype(BF16), keys)
        ps = []
        for half in range(2):
            quads = []
            for r in range(2):
                sm = s[r * WINDOW:(r + 1) * WINDOW, half * 2 * WINDOW:(half + 1) * 2 * WINDOW] + bias
                sink = sinks_ref[layer, SWA_GROUP * kvh + 2 * r + half]
                quads.append(_sink_softmax_rows(sm, sink).astype(BF16))
            ps.append(jnp.concatenate(quads, axis=0))
        og = _dot(jnp.concatenate(ps, axis=1), vals)
        outs.extend([og[:WINDOW], og[WINDOW:]])
    for a in range(2 * SWA_KV_HEADS):
        kprep[a] = kc[a]
        vprep[a] = vc[a]
    return jnp.concatenate(outs, axis=1), k, v


def _hgrn_chunk(hq, hf, v, hg, lb, onorm, tri, lvl, s_scr):
    c = HG_CHUNK
    q, g, kg = _hgrn_gates(hq, hf, lb)
    g_hi, g_mid, g_lo = _split3(g)
    G = _dot(tri, g_hi) + _dot(tri, g_mid) + _dot(tri, g_lo)
    t_col = lax.broadcasted_iota(jnp.int32, (c, 1), 0)
    levels = []
    half = HG_DIAG
    while half < c:
        blk = 2 * half
        gb = jnp.concatenate(
            [jnp.broadcast_to(G[p * blk + half - 1:p * blk + half, :], (blk, G.shape[1]))
             for p in range(c // blk)], axis=0)
        second = (t_col & half) != 0
        e = jnp.exp(jnp.where(second, G - gb, gb - G))
        levels.append(((q * e).astype(BF16), (kg * e).astype(BF16), lvl == len(levels)))
        half = blk
    g_last = G[c - 1:c, :]
    q_in = (q * jnp.exp(G)).astype(BF16)
    k_out = (kg * jnp.exp(jnp.minimum(g_last - G, 0.0))).astype(BF16)
    dec_last = jnp.exp(g_last)
    v_bf = v.astype(BF16)
    nblk = c // HG_DIAG
    row_in_blk = lax.broadcasted_iota(jnp.int32, (nblk, HG_DIAG, HG_DV), 1)
    o_heads = []
    for h in range(HG_HEADS):
        hs = slice(h * HG_DK, (h + 1) * HG_DK)
        s_h = s_scr[h]
        a_off = jnp.zeros((c, c), F32)
        for qt, kt, same in levels:
            a_off = a_off + jnp.where(same, _dot_nt(qt[:, hs], kt[:, hs]), 0.0)
        o_h = _dot(q_in[:, hs], s_h.astype(BF16)) + _dot(a_off.astype(BF16), v_bf[:, hs])
        q3 = q[:, hs].reshape(nblk, HG_DIAG, HG_DK)
        k3 = kg[:, hs].reshape(nblk, HG_DIAG, HG_DK)
        g3 = G[:, hs].reshape(nblk, HG_DIAG, HG_DK)
        v3 = v[:, hs].reshape(nblk, HG_DIAG, HG_DV)
        od = jnp.zeros((nblk, HG_DIAG, HG_DV), F32)
        for s in range(HG_DIAG):
            e = jnp.exp(jnp.minimum(g3 - g3[:, s:s + 1, :], 0.0))
            a = jnp.sum(q3 * (k3[:, s:s + 1, :] * e), axis=-1, keepdims=True)
            od = od + a * jnp.where(row_in_blk >= s, v3[:, s:s + 1, :], 0.0)
        o_heads.append(o_h + od.reshape(c, HG_DV))
        s_scr[h] = _col_bcast(dec_last[:, hs]) * s_h + _dot_tn(k_out[:, hs], v_bf[:, hs])
    return _hgrn_out(o_heads, onorm, hg)


def _pool_tile(u, carry, pos0, pw_ref, ps_ref):
    tt = u.shape[0]
    x = jnp.concatenate([carry[...], u], axis=0)
    sums = []
    cur = x
    shift = 1
    for gi in range(len(POOL_WINDOWS)):
        cur = cur + pltpu.roll(cur, shift, 0)
        sums.append(cur[POOL_CARRY:, :POOL_GROUP_DIM])
        cur = cur[:, POOL_GROUP_DIM:]
        shift *= 2
    pos = pos0 + lax.broadcasted_iota(jnp.int32, (tt, 1), 0)
    d_groups = []
    for gi, w in enumerate(POOL_WINDOWS):
        cs = slice(gi * POOL_GROUP_DIM, (gi + 1) * POOL_GROUP_DIM)
        cnt = jnp.minimum(w, pos + 1).astype(F32)
        d_groups.append(sums[gi] / cnt - u[:, cs])
    carry[...] = u[tt - POOL_CARRY:, :]
    return _pool_project(d_groups, pw_ref, ps_ref)


def _layer_mix_kernel(sinks_ref, x_ref, xnext_ref, nw_ref, w_ref, cos_ref, sin_ref, qn_ref, kn_ref, bd_ref,
                      bias_ref, lb_ref, on_ref, tri_ref, lvl_ref, pw_ref, ps_ref, wb_ref, wo_ref,
                      x1_ref, kc_ref, vc_ref, s_out_ref, ulast_ref,
                      z_a, z_b, kprep, vprep, s_scr, carry, *, layer, steps_per_seq):
    s = pl.program_id(0)
    n = s % steps_per_seq
    nw = nw_ref[...]

    @pl.when(s == 0)
    def _():
        _project_tile(x_ref[:MIX_TILE, :], nw, w_ref, z_a)

    @pl.when(n == 0)
    def _():
        kprep[...] = jnp.zeros_like(kprep)
        vprep[...] = jnp.zeros_like(vprep)
        s_scr[...] = jnp.zeros_like(s_scr)
        carry[...] = jnp.zeros_like(carry)

    qn = qn_ref[...]
    kn = kn_ref[...]
    bd = bd_ref[...]
    lb = lb_ref[...]
    onorm = on_ref[...]
    tri = tri_ref[...]
    lvl = lvl_ref[...]
    band = bias_ref[1]
    first = bias_ref[jnp.minimum(n, 1)]

    def mix(z_ref, t):
        r0 = t * MIX_TILE
        o_a, o_b = [], []
        for j in range(MIX_TILE // WINDOW):
            rz = slice(j * WINDOW, (j + 1) * WINDOW)
            rx = slice(r0 + j * WINDOW, r0 + (j + 1) * WINDOW)
            ob, k, v = _swa_block(z_ref[rz, 0:P3], cos_ref[rx, :], sin_ref[rx, :], qn, kn, bd,
                                  first if (t == 0 and j == 0) else band, sinks_ref, layer, kprep, vprep)
            o_b.append(ob)
            kc_ref[...] = k
            vc_ref[...] = v
            o_a.append(_hgrn_chunk(z_ref[rz, P3:P4], z_ref[rz, P4:P5], z_ref[rz, P5:P6], z_ref[rz, P6:P7],
                                   lb, onorm, tri, lvl, s_scr))
        u = z_ref[:, P7:P8]
        ulast_ref[...] = u[MIX_TILE - POOL_CARRY:, :]
        o_c = _pool_tile(u, carry, n * (2 * MIX_TILE) + r0, pw_ref, ps_ref)
        m = None
        for nb, br in enumerate((jnp.concatenate(o_a, axis=0), jnp.concatenate(o_b, axis=0), o_c)):
            gate = _sigmoid(z_ref[:, P8 + nb * D_MODEL:P8 + (nb + 1) * D_MODEL])
            term = gate * _dot(br.astype(BF16), wb_ref[nb])
            m = term if m is None else m + term
        x1_ref[r0:r0 + MIX_TILE, :] = x_ref[r0:r0 + MIX_TILE, :] + _dot(m.astype(BF16), wo_ref[...])

    _project_tile(x_ref[MIX_TILE:, :], nw, w_ref, z_b)
    mix(z_a, 0)
    _project_tile(xnext_ref[...], nw, w_ref, z_a)
    mix(z_b, 1)
    for h in range(HG_HEADS):
        s_out_ref[h] = s_scr[h]


def _layer_mix(x, norm_w, w_in_bf, cos, sin, qn, kn, bd, bias, sinks, lower, onorm, tri, lvl, pool_w_bf, pool_scale,
               w_branch_bf, w_o_bf, layer, batch, seq):
    step_rows = 2 * MIX_TILE
    sps = seq // step_rows
    nsteps = batch * sps
    last_tile = batch * seq // MIX_TILE - 1
    const2 = lambda s: (0, 0)
    lay3 = lambda s: (layer, 0, 0)
    per_seq = lambda s: (s // sps, 0, 0)
    single = pl.Buffered(1)
    return pl.pallas_call(
        functools.partial(_layer_mix_kernel, layer=layer, steps_per_seq=sps),
        grid=(nsteps,),
        in_specs=[
            pl.BlockSpec(memory_space=pltpu.SMEM),
            pl.BlockSpec((step_rows, D_MODEL), lambda s: (s, 0)),
            pl.BlockSpec((MIX_TILE, D_MODEL), lambda s: (jnp.minimum(2 * s + 2, last_tile), 0)),
            pl.BlockSpec((None, 1, D_MODEL), lay3),
            pl.BlockSpec((None, D_MODEL, D_IN), lay3, pipeline_mode=single),
            pl.BlockSpec((step_rows, 128), lambda s: (s % sps, 0)),
            pl.BlockSpec((step_rows, 128), lambda s: (s % sps, 0)),
            pl.BlockSpec((1, SWA_Q), const2),
            pl.BlockSpec((1, SWA_KV), const2),
            pl.BlockSpec((256, 256), const2),
            pl.BlockSpec((2, WINDOW, 2 * WINDOW), lambda s: (0, 0, 0)),
            pl.BlockSpec((None, 1, HG_K), lay3),
            pl.BlockSpec((None, 1, HG_DV), lay3),
            pl.BlockSpec((HG_CHUNK, HG_CHUNK), const2),
            pl.BlockSpec((HG_CHUNK, HG_CHUNK), const2),
            pl.BlockSpec((None, len(POOL_WINDOWS), POOL_GROUP_DIM, POOL_GROUP_DIM), lambda s: (layer, 0, 0, 0)),
            pl.BlockSpec((None, 1, POOL_WIDTH), lay3),
            pl.BlockSpec((None, N_BRANCH, BRANCH_WIDTH, D_MODEL), lambda s: (layer, 0, 0, 0), pipeline_mode=single),
            pl.BlockSpec((None, D_MODEL, D_MODEL), lay3, pipeline_mode=single),
        ],
        out_specs=[
            pl.BlockSpec((step_rows, D_MODEL), lambda s: (s, 0)),
            pl.BlockSpec((None, WINDOW, SWA_KV), per_seq),
            pl.BlockSpec((None, WINDOW, SWA_KV), per_seq),
            pl.BlockSpec((None, HG_HEADS, HG_DK, HG_DV), lambda s: (s // sps, 0, 0, 0)),
            pl.BlockSpec((None, POOL_CARRY, POOL_WIDTH), per_seq),
        ],
        out_shape=[
            jax.ShapeDtypeStruct((batch * seq, D_MODEL), F32),
            jax.ShapeDtypeStruct((batch, WINDOW, SWA_KV), F32),
            jax.ShapeDtypeStruct((batch, WINDOW, SWA_KV), F32),
            jax.ShapeDtypeStruct((batch, HG_HEADS, HG_DK, HG_DV), F32),
            jax.ShapeDtypeStruct((batch, POOL_CARRY, POOL_WIDTH), F32),
        ],
        scratch_shapes=[
            pltpu.VMEM((MIX_TILE, D_IN), F32),
            pltpu.VMEM((MIX_TILE, D_IN), F32),
            pltpu.VMEM((2 * SWA_KV_HEADS, WINDOW, 128), BF16),
            pltpu.VMEM((2 * SWA_KV_HEADS, WINDOW, 128), BF16),
            pltpu.VMEM((HG_HEADS, HG_DK, HG_DV), F32),
            pltpu.VMEM((POOL_CARRY, POOL_WIDTH), F32),
        ],
        compiler_params=_cparams("arbitrary"),
        name="layer_mix",
    )(sinks, x, x, norm_w, w_in_bf, cos, sin, qn, kn, bd, bias, lower, onorm, tri, lvl, pool_w_bf, pool_scale,
      w_branch_bf, w_o_bf)


def _ffn_kernel(x_ref, nw_ref, wu_ref, wd_ref, o_ref):
    x = x_ref[...]
    xn = _rms_rows(x, nw_ref[...]).astype(BF16)
    acc = x
    step = D_MODEL
    for c in range(0, D_FF, step):
        h = _dot(xn, wu_ref[:, c:c + step])
        h = jnp.square(jnp.maximum(h, 0.0)).astype(BF16)
        acc = acc + _dot(h, wd_ref[c:c + step, :])
    o_ref[...] = acc


def _ffn(x, norm_w, w_up_bf, w_down_bf, layer, tm):
    m = x.shape[0]
    row = lambda i: (i, 0)
    return pl.pallas_call(
        _ffn_kernel,
        grid=(m // tm,),
        in_specs=[
            pl.BlockSpec((tm, D_MODEL), row),
            pl.BlockSpec((None, 1, D_MODEL), lambda i: (layer, 0, 0)),
            pl.BlockSpec((None, D_MODEL, D_FF), lambda i: (layer, 0, 0), pipeline_mode=pl.Buffered(1)),
            pl.BlockSpec((None, D_FF, D_MODEL), lambda i: (layer, 0, 0), pipeline_mode=pl.Buffered(1)),
        ],
        out_specs=pl.BlockSpec((tm, D_MODEL), row),
        out_shape=jax.ShapeDtypeStruct((m, D_MODEL), F32),
        compiler_params=_cparams("arbitrary"),
        name="ffn",
    )(x, norm_w, w_up_bf, w_down_bf)


def _rope_tables(pos):
    hd = SWA_HEAD_DIM
    inv = jnp.power(ROPE_THETA, -jnp.arange(0, hd, 2, dtype=F32) / hd)
    ang = pos[:, None] * inv[None, :]
    cos = jnp.cos(ang)
    sin = jnp.sin(ang)
    cos_t = jnp.concatenate([cos, cos, cos, cos], axis=1)
    sin_t = jnp.concatenate([-sin, sin, -sin, sin], axis=1)
    return cos_t, sin_t


def _hgrn_levels():
    t = jnp.arange(HG_CHUNK)[:, None]
    s = jnp.arange(HG_CHUNK)[None, :]
    lvl = jnp.full((HG_CHUNK, HG_CHUNK), -1, jnp.int32)
    half, level = HG_DIAG, 0
    while half < HG_CHUNK:
        shift = (2 * half).bit_length() - 1
        own = ((t >> shift) == (s >> shift)) & ((t & half) != 0) & ((s & half) == 0)
        lvl = jnp.where(own, level, lvl)
        half, level = 2 * half, level + 1
    return lvl


def _prompt_bias():
    i = jnp.arange(WINDOW)[:, None]
    j = jnp.arange(2 * WINDOW)[None, :]
    band = (j > i) & (j <= i + WINDOW)
    first = band & (j >= WINDOW)
    return jnp.where(jnp.stack([first, band]), 0.0, NEG).astype(F32)


def _sample_bias(wb, t_new):
    grp = 8 // t_new
    n_cache = grp * wb
    nkeys = -(-(n_cache + 8) // 128) * 128
    r = jnp.arange(8)[:, None]
    seq_r, t = r // t_new, r % t_new
    j = jnp.arange(nkeys)[None, :]
    cache_ok = (j < n_cache) & (j // wb == seq_r) & (wb + t - j % wb < WINDOW)
    jn = j - n_cache
    new_ok = (jn >= 0) & (jn < 8) & (jn // t_new == seq_r) & (jn % t_new <= t)
    return jnp.where(cache_ok | new_ok, 0.0, NEG).astype(F32)


def _row_tile(m, cap):
    t = min(m, cap)
    while m % t:
        t //= 2
    return t


def kernel(x_prompt, x_sample, state_hgrn, cache_swa_k, cache_swa_v, state_pool, norm_mix, w_in, q_norm, k_norm,
           attn_sinks, hgrn_lb, hgrn_onorm, pool_w, pool_scale, w_branch, w_o, norm_ffn, w_up, w_down):
    depth = w_in.shape[0]
    bp, seq, _ = x_prompt.shape
    bs, t_new, _ = x_sample.shape
    wb = cache_swa_k.shape[2]

    w_in_bf = w_in.astype(BF16)
    w_branch_bf = w_branch.astype(BF16)
    w_o_bf = w_o.astype(BF16)
    w_up_bf = w_up.astype(BF16)
    w_down_bf = w_down.astype(BF16)
    pool_w_bf = pool_w.astype(BF16)
    norm_mix3 = norm_mix.reshape(depth, 1, D_MODEL)
    norm_ffn3 = norm_ffn.reshape(depth, 1, D_MODEL)
    onorm3 = hgrn_onorm.reshape(depth, 1, HG_DV)
    pool_scale3 = pool_scale.reshape(depth, 1, POOL_WIDTH)
    lower3 = _lower_bounds(hgrn_lb).reshape(depth, 1, HG_K)
    cache_k4 = cache_swa_k.reshape(depth, bs, wb, SWA_KV)
    cache_v4 = cache_swa_v.reshape(depth, bs, wb, SWA_KV)

    cos_p, sin_p = _rope_tables(jnp.arange(seq, dtype=F32))
    cos_s, sin_s = _rope_tables(jnp.arange(t_new, dtype=F32) + PAST_LEN)
    bb_swa = SWA_SAMPLE_BB if bs % SWA_SAMPLE_BB == 0 else bs
    cos_s = jnp.tile(cos_s, (bb_swa, 1))
    sin_s = jnp.tile(sin_s, (bb_swa, 1))
    bias_p = _prompt_bias()
    bias_s = _sample_bias(wb, t_new)
    lane = jnp.arange(256)
    bd = (lane[:, None] // SWA_HEAD_DIM == lane[None, :] // SWA_HEAD_DIM).astype(BF16)
    tok = jnp.arange(HG_CHUNK)
    tri = (tok[:, None] >= tok[None, :]).astype(BF16)
    lvl = _hgrn_levels()

    hp = x_prompt.reshape(bp * seq, D_MODEL)
    hs = x_sample.reshape(bs * t_new, D_MODEL)
    tm_p = _row_tile(bp * seq, 512)
    tm_s = _row_tile(bs * t_new, 512)

    sp = ([], [], [], [])
    ss = ([], [], [], [])
    kv_stacks = (cache_k4, cache_v4)
    for l in range(depth):
        qn = jnp.tile(q_norm[l], SWA_HEADS).reshape(1, SWA_Q)
        kn = jnp.tile(k_norm[l], SWA_KV_HEADS).reshape(1, SWA_KV)

        x1, k_last, v_last, s_new, u_last = _layer_mix(
            hp, norm_mix3, w_in_bf, cos_p, sin_p, qn, kn, bd, bias_p, attn_sinks, lower3, onorm3, tri, lvl,
            pool_w_bf, pool_scale3, w_branch_bf, w_o_bf, l, bp, seq)
        hp = _ffn(x1, norm_ffn3, w_up_bf, w_down_bf, l, tm_p)
        sp[0].append(s_new)
        sp[1].append(k_last.reshape(bp, WINDOW, SWA_KV_HEADS, SWA_HEAD_DIM))
        sp[2].append(v_last.reshape(bp, WINDOW, SWA_KV_HEADS, SWA_HEAD_DIM))
        sp[3].append(u_last[:, POOL_CARRY - POOL_BUF:])

        qkv, hq, hf, hi, hg, u, zg = _proj_in(hs, norm_mix3, w_in_bf, l, tm_s)
        o_b, k_stack, v_stack = _swa_sample(qkv, cos_s, sin_s, qn, kn, bd, bias_s, attn_sinks, cache_k4, cache_v4,
                                            kv_stacks, l, bs, t_new)
        kv_stacks = (k_stack, v_stack)
        o_a, s_new = _hgrn_sample(hq, hf, hi, hg, lower3, onorm3, state_hgrn, l, bs, t_new)
        ss[0].append(s_new)
        u3 = u.reshape(bs, t_new, POOL_WIDTH)
        o_c = _pool_sample(jnp.swapaxes(state_pool[l], 0, 1), jnp.swapaxes(u3, 0, 1),
                           pool_w_bf, pool_scale3, l, t_new)
        o_c = jnp.swapaxes(o_c, 0, 1).reshape(bs * t_new, POOL_WIDTH)
        x1 = _merge(hs, o_a, o_b, o_c, zg, w_branch_bf, w_o_bf, l, tm_s)
        hs = _ffn(x1, norm_ffn3, w_up_bf, w_down_bf, l, tm_s)
        ss[3].append(jnp.concatenate([state_pool[l], u3], axis=1)[:, -POOL_BUF:])

    cache_shape = (depth, bs, wb, SWA_KV_HEADS, SWA_HEAD_DIM)
    return (hp.reshape(bp, seq, D_MODEL), hs.reshape(bs, t_new, D_MODEL),
            jnp.stack(sp[0]), jnp.stack(sp[1]), jnp.stack(sp[2]), jnp.stack(sp[3]),
            jnp.stack(ss[0]), kv_stacks[0].reshape(cache_shape), kv_stacks[1].reshape(cache_shape),
            jnp.stack(ss[3]))
```

```python
import functools

import jax
import jax.numpy as jnp
from jax import lax
from jax.experimental import pallas as pl
from jax.experimental.pallas import tpu as pltpu

F32 = jnp.float32
BF16 = jnp.bfloat16

D_MODEL = 1024
PAST_LEN = 8192
SWA_HEADS = 8
SWA_KV_HEADS = 2
SWA_HEAD_DIM = 64
SWA_GROUP = SWA_HEADS // SWA_KV_HEADS
WINDOW = 128
ROPE_THETA = 10000.0
HG_HEADS = 4
HG_DK = 128
HG_DV = 128
LB_FLOOR = 1e-30
POOL_WINDOWS = (2, 4, 8, 16)
POOL_GROUP_DIM = 128
POOL_WIDTH = 512
POOL_BUF = 15
BRANCH_WIDTH = 512
N_BRANCH = 3
D_FF = 4 * D_MODEL
EPS = 1e-6
NEG = -1e30
LOG2E = 1.4426950408889634

SWA_Q = SWA_HEADS * SWA_HEAD_DIM
SWA_KV = SWA_KV_HEADS * SWA_HEAD_DIM
HG_K = HG_HEADS * HG_DK
QKV_W = SWA_Q + 2 * SWA_KV
P3 = QKV_W
P4 = P3 + HG_K
P5 = P4 + HG_K
P6 = P5 + HG_K
P7 = P6 + HG_K
P8 = P7 + POOL_WIDTH
D_IN = P8 + N_BRANCH * D_MODEL

VMEM_LIMIT_BYTES = 56 * 1024 * 1024
HG_CHUNK = 128
HG_DIAG = 8
POOL_CARRY = 16
SAMPLE_BB = 8
MIX_TILE = 256
SWA_SAMPLE_BB = 16


def _cparams(*sem):
    return pltpu.CompilerParams(dimension_semantics=sem, vmem_limit_bytes=VMEM_LIMIT_BYTES)


def _rms_rows(x, w):
    ms = jnp.mean(x * x, axis=-1, keepdims=True)
    return x * lax.rsqrt(ms + EPS) * w


def _sigmoid(z):
    return 1.0 / (1.0 + jnp.exp2(z * (-LOG2E)))


def _silu(z):
    return z * _sigmoid(z)


def _split3(x):
    hi = x.astype(BF16)
    r1 = x - hi.astype(F32)
    mid = r1.astype(BF16)
    lo = (r1 - mid.astype(F32)).astype(BF16)
    return hi, mid, lo


def _dot(a, b):
    return jnp.dot(a, b, preferred_element_type=F32)


def _dot_nt(a, b):
    return lax.dot_general(a, b, (((1,), (1,)), ((), ())), preferred_element_type=F32)


def _dot_tn(a, b):
    return lax.dot_general(a, b, (((0,), (0,)), ((), ())), preferred_element_type=F32)


def _lower_kernel(lb_ref, o_ref):
    x = lb_ref[...]
    depth = x.shape[0]
    rows = [x[i:i + 1] for i in range(depth)]
    m = rows[0]
    for r in rows[1:]:
        m = jnp.maximum(m, r)
    es = [jnp.exp(r - m) for r in rows]
    tot = es[0]
    for e in es[1:]:
        tot = tot + e
    ps = [e / tot for e in es]
    out = []
    c = None
    for p in ps:
        c = p if c is None else c + p
        out.append(jnp.maximum(c - ps[0], 0.0))
    o_ref[...] = jnp.concatenate(out, axis=0)


def _lower_bounds(hgrn_lb):
    return pl.pallas_call(
        _lower_kernel,
        out_shape=jax.ShapeDtypeStruct(hgrn_lb.shape, F32),
        name="lower_bounds",
    )(hgrn_lb)


def _proj_in_kernel(x_ref, nw_ref, w_ref, qkv_ref, hq_ref, hf_ref, hi_ref, hg_ref, u_ref, zg_ref):
    xn = _rms_rows(x_ref[...], nw_ref[...]).astype(BF16)

    def mm(a, b):
        return _dot(xn, w_ref[:, a:b])

    qkv_ref[...] = mm(0, P3)
    hq_ref[...] = mm(P3, P4)
    hf_ref[...] = mm(P4, P5)
    hi_ref[...] = mm(P5, P6)
    hg_ref[...] = mm(P6, P7)
    u_ref[...] = mm(P7, P8)
    for n in range(N_BRANCH):
        zg_ref[:, n * D_MODEL:(n + 1) * D_MODEL] = mm(P8 + n * D_MODEL, P8 + (n + 1) * D_MODEL)


def _proj_in(x, norm_w, w_in_bf, layer, tm):
    m = x.shape[0]
    widths = (QKV_W, HG_K, HG_K, HG_K, HG_K, POOL_WIDTH, N_BRANCH * D_MODEL)
    return pl.pallas_call(
        _proj_in_kernel,
        grid=(m // tm,),
        in_specs=[
            pl.BlockSpec((tm, D_MODEL), lambda i: (i, 0)),
            pl.BlockSpec((None, 1, D_MODEL), lambda i: (layer, 0, 0)),
            pl.BlockSpec((None, D_MODEL, D_IN), lambda i: (layer, 0, 0), pipeline_mode=pl.Buffered(1)),
        ],
        out_specs=[pl.BlockSpec((tm, w), lambda i: (i, 0)) for w in widths],
        out_shape=[jax.ShapeDtypeStruct((m, w), F32) for w in widths],
        compiler_params=_cparams("arbitrary"),
        name="proj_in",
    )(x, norm_w, w_in_bf)


def _head_norm_rope(x, w_row, cos, sin, bd):
    width = x.shape[1]
    sq = x * x
    hi = sq.astype(BF16)
    lo = (sq - hi.astype(F32)).astype(BF16)
    parts = []
    for c in range(0, width, 256):
        wc = min(256, width - c)
        b = bd[:wc, :wc]
        parts.append(_dot(hi[:, c:c + wc], b) + _dot(lo[:, c:c + wc], b))
    ms = (parts[0] if len(parts) == 1 else jnp.concatenate(parts, axis=1)) * (1.0 / SWA_HEAD_DIM)
    y = x * lax.rsqrt(ms + EPS) * w_row
    reps = width // 128
    cos_t = cos if reps == 1 else jnp.concatenate([cos] * reps, axis=1)
    sin_t = sin if reps == 1 else jnp.concatenate([sin] * reps, axis=1)
    lane = lax.broadcasted_iota(jnp.int32, y.shape, 1)
    first_half = (lane & (SWA_HEAD_DIM - 1)) < (SWA_HEAD_DIM // 2)
    half = SWA_HEAD_DIM // 2
    rot = jnp.where(first_half, pltpu.roll(y, width - half, 1), pltpu.roll(y, half, 1))
    return y * cos_t + rot * sin_t


def _kv_pad(x):
    lo = lax.broadcasted_iota(jnp.int32, x.shape, 1) < SWA_HEAD_DIM
    xr = pltpu.roll(x, SWA_HEAD_DIM, 1)
    return (jnp.where(lo, x, 0.0), jnp.where(lo, 0.0, xr), jnp.where(lo, xr, 0.0), jnp.where(lo, 0.0, x))


def _sink_softmax_rows(sm, sink):
    mx = jnp.maximum(jnp.max(sm, axis=-1, keepdims=True), sink)
    e = jnp.exp(sm - mx)
    den = jnp.sum(e, axis=-1, keepdims=True) + jnp.exp(sink - mx)
    return e * (1.0 / den)


def _swa_sample_kernel(sinks_ref, qkv_ref, cos_ref, sin_ref, qn_ref, kn_ref, bd_ref, bias_ref, kb_ref, vb_ref,
                       kstack_ref, vstack_ref, o_ref, kc_out_ref, vc_out_ref, *, layer, t_new):
    del kstack_ref, vstack_ref
    z = qkv_ref[...]
    cos = cos_ref[...]
    sin = sin_ref[...]
    bd = bd_ref[...]
    q = _head_norm_rope(z[:, :SWA_Q], qn_ref[...], cos, sin, bd) * (SWA_HEAD_DIM ** -0.5)
    k = _head_norm_rope(z[:, SWA_Q:SWA_Q + SWA_KV], kn_ref[...], cos, sin, bd)
    v = z[:, SWA_Q + SWA_KV:]

    rows = z.shape[0]
    ng = rows // 8
    nkeys = bias_ref.shape[1]
    nseq, wb = kb_ref.shape[0], kb_ref.shape[1]
    for cache_ref, out_ref, new in ((kb_ref, kc_out_ref, k), (vb_ref, vc_out_ref, v)):
        flat = cache_ref[...].reshape(nseq * wb, SWA_KV)
        out_ref[...] = pltpu.roll(flat, nseq * wb - t_new, 0).reshape(nseq, wb, SWA_KV)
        for b in range(nseq):
            out_ref[b, wb - t_new:wb, :] = new[b * t_new:(b + 1) * t_new]
    lo = lax.broadcasted_iota(jnp.int32, (rows, 128), 1) < SWA_HEAD_DIM
    qw = []
    for h in range(SWA_HEADS):
        qc = q[:, (h // 2) * 128:(h // 2 + 1) * 128]
        want_hi = h // SWA_GROUP == 1
        if (h % 2 == 1) != want_hi:
            qc = pltpu.roll(qc, SWA_HEAD_DIM, 1)
        qw.append((jnp.where(lo, 0.0, qc) if want_hi else jnp.where(lo, qc, 0.0)).reshape(ng, 8, 128))
    qw = jnp.concatenate(qw, axis=1).astype(BF16)

    def keys_of(cache_ref, new):
        cache = cache_ref[...].reshape(ng, (8 // t_new) * wb, SWA_KV)
        pad = jnp.zeros((ng, nkeys - cache.shape[1] - 8, SWA_KV), F32)
        return jnp.concatenate([cache, new.reshape(ng, 8, SWA_KV), pad], axis=1).astype(BF16)

    s = jnp.einsum('gqd,gkd->gqk', qw, keys_of(kb_ref, k), preferred_element_type=F32)
    bias = bias_ref[...]
    p = jnp.concatenate(
        [_sink_softmax_rows(s[:, h * 8:(h + 1) * 8, :] + bias, sinks_ref[layer, h]).astype(BF16)
         for h in range(SWA_HEADS)], axis=1)
    res = jnp.einsum('gqk,gkd->gqd', p, keys_of(vb_ref, v), preferred_element_type=F32)
    res = res.reshape(ng * SWA_HEADS * 8, 128)
    res_r = pltpu.roll(res, SWA_HEAD_DIM, 1)
    res = res.reshape(ng, SWA_HEADS * 8, 128)
    res_r = res_r.reshape(ng, SWA_HEADS * 8, 128)
    lo3 = lo.reshape(ng, 8, 128)
    cols = []
    for c in range(SWA_HEADS // 2):
        kv_hi = (2 * c) // SWA_GROUP == 1
        even = (res_r if kv_hi else res)[:, (2 * c) * 8:(2 * c + 1) * 8, :]
        odd = (res if kv_hi else res_r)[:, (2 * c + 1) * 8:(2 * c + 2) * 8, :]
        cols.append(jnp.where(lo3, even, odd).reshape(rows, 128))
    o_ref[...] = jnp.concatenate(cols, axis=1)


def _swa_sample(qkv, cos, sin, qn, kn, bd, bias, sinks, kbuf, vbuf, stacked_prev, layer, batch, t_new):
    bb = SWA_SAMPLE_BB if batch % SWA_SAMPLE_BB == 0 else batch
    rb = bb * t_new
    wb = kbuf.shape[2]
    row = lambda i: (i, 0)
    const = lambda i: (0, 0)
    cache_blk = pl.BlockSpec((None, bb, wb, SWA_KV), lambda i: (layer, i, 0, 0))
    any_spec = pl.BlockSpec(memory_space=pl.ANY)
    return pl.pallas_call(
        functools.partial(_swa_sample_kernel, layer=layer, t_new=t_new),
        grid=(batch // bb,),
        in_specs=[
            pl.BlockSpec(memory_space=pltpu.SMEM),
            pl.BlockSpec((rb, QKV_W), row),
            pl.BlockSpec((rb, 128), const),
            pl.BlockSpec((rb, 128), const),
            pl.BlockSpec((1, SWA_Q), const),
            pl.BlockSpec((1, SWA_KV), const),
            pl.BlockSpec((256, 256), const),
            pl.BlockSpec(bias.shape, const),
            cache_blk,
            cache_blk,
            any_spec,
            any_spec,
        ],
        out_specs=[pl.BlockSpec((rb, SWA_Q), row), cache_blk, cache_blk],
        out_shape=[
            jax.ShapeDtypeStruct((batch * t_new, SWA_Q), F32),
            jax.ShapeDtypeStruct(kbuf.shape, F32),
            jax.ShapeDtypeStruct(vbuf.shape, F32),
        ],
        input_output_aliases={10: 1, 11: 2},
        compiler_params=_cparams("arbitrary"),
        name="swa_sample",
    )(sinks, qkv, cos, sin, qn, kn, bd, bias, kbuf, vbuf, stacked_prev[0], stacked_prev[1])


def _hgrn_gates(hq, hf, lb):
    sig_p = _sigmoid(hf)
    om = 1.0 - lb
    log2f = jnp.log(jnp.maximum(lb, LB_FLOOR) + om * sig_p) * LOG2E
    return _silu(hq), log2f, om * (1.0 - sig_p)


def _hgrn_out(o, onorm, hg):
    outs = []
    for h in range(HG_HEADS):
        hs = slice(h * HG_DV, (h + 1) * HG_DV)
        outs.append(_rms_rows(o[h], onorm) * _silu(hg[:, hs]))
    return jnp.concatenate(outs, axis=1)


def _col_bcast(row):
    return jnp.transpose(jnp.broadcast_to(row, (row.shape[1], row.shape[1])))


def _hgrn_sample_kernel(hq_ref, hf_ref, hi_ref, hg_ref, lb_ref, on_ref, s0_ref, o_ref, s_out_ref, *, t_new):
    q, g, kg = _hgrn_gates(hq_ref[...], hf_ref[...], lb_ref[...])
    v = hi_ref[...]
    rows = q.shape[0]
    t_idx = lax.broadcasted_iota(jnp.int32, (rows, 1), 0) % t_new

    def down(x, d):
        return pltpu.roll(x, d, 0)

    def up(x, d):
        return pltpu.roll(x, rows - d, 0)

    G = g
    for d in range(1, t_new):
        G = G + jnp.where(t_idx >= d, down(g, d), 0.0)
    rest = jnp.zeros_like(g)
    for d in range(1, t_new):
        rest = rest + jnp.where(t_idx + d < t_new, up(g, d), 0.0)
    q_in = q * jnp.exp2(G)
    k_out = kg * jnp.exp2(rest)
    dec_all = jnp.exp2(G + rest)

    o_intra = [jnp.zeros((rows, HG_DV), F32) for _ in range(HG_HEADS)]
    for d in range(t_new):
        kd = kg if d == 0 else down(kg, d)
        vd = v if d == 0 else down(v, d)
        gd = G if d == 0 else down(G, d)
        w = q * kd * jnp.exp2(jnp.minimum(G - gd, 0.0))
        for h in range(HG_HEADS):
            hs = slice(h * HG_DK, (h + 1) * HG_DK)
            a = jnp.sum(w[:, hs], axis=-1, keepdims=True)
            o_intra[h] = o_intra[h] + jnp.where(t_idx >= d, a, 0.0) * vd[:, hs]

    dec_hi = dec_all.astype(BF16).astype(F32)
    dec_r = dec_all - dec_hi
    dec_mid = dec_r.astype(BF16).astype(F32)
    dec_lo = (dec_r - dec_mid).astype(BF16).astype(F32)
    dec_parts = jnp.where(t_idx == 0, dec_hi, jnp.where(t_idx == 1, dec_mid, jnp.where(t_idx == 2, dec_lo, 0.0)))
    seq_of_row = lax.broadcasted_iota(jnp.int32, (rows, 1), 0) // t_new
    ones = jnp.ones((rows, HG_DV), BF16)
    v_bf = v.astype(BF16)
    o = []
    for h in range(HG_HEADS):
        hs = slice(h * HG_DK, (h + 1) * HG_DK)
        o_h = o_intra[h]
        for b in range(s0_ref.shape[0]):
            mine = seq_of_row == b
            s_h = s0_ref[b, h]
            o_h = o_h + _dot(jnp.where(mine, q_in[:, hs], 0.0).astype(BF16), s_h.astype(BF16))
            dec_mat = _dot_tn(jnp.where(mine, dec_parts[:, hs], 0.0).astype(BF16), ones)
            upd = _dot_tn(jnp.where(mine, k_out[:, hs], 0.0).astype(BF16), v_bf[:, hs])
            s_out_ref[b, h] = dec_mat * s_h + upd
        o.append(o_h)
    o_ref[...] = _hgrn_out(o, on_ref[...], hg_ref[...])


def _hgrn_sample(hq, hf, hi, hg, lower, onorm, state, layer, batch, t_new):
    assert t_new >= 3
    rb = SAMPLE_BB * t_new
    blk = pl.BlockSpec((rb, HG_K), lambda i: (i, 0))
    return pl.pallas_call(
        functools.partial(_hgrn_sample_kernel, t_new=t_new),
        grid=(batch // SAMPLE_BB,),
        in_specs=[
            blk, blk, blk, blk,
            pl.BlockSpec((None, 1, HG_K), lambda i: (layer, 0, 0)),
            pl.BlockSpec((None, 1, HG_DV), lambda i: (layer, 0, 0)),
            pl.BlockSpec((None, SAMPLE_BB, HG_HEADS, HG_DK, HG_DV), lambda i: (layer, i, 0, 0, 0)),
        ],
        out_specs=[blk, pl.BlockSpec((SAMPLE_BB, HG_HEADS, HG_DK, HG_DV), lambda i: (i, 0, 0, 0))],
        out_shape=[
            jax.ShapeDtypeStruct((batch * t_new, HG_K), F32),
            jax.ShapeDtypeStruct((batch, HG_HEADS, HG_DK, HG_DV), state.dtype),
        ],
        compiler_params=_cparams("arbitrary"),
        name="hgrn_sample",
    )(hq, hf, hi, hg, lower, onorm, state)


def _pool_project(d_groups, pw_ref, ps_ref):
    outs = []
    for gi, d in enumerate(d_groups):
        cs = slice(gi * POOL_GROUP_DIM, (gi + 1) * POOL_GROUP_DIM)
        outs.append(_dot(d.astype(BF16), pw_ref[gi]) * ps_ref[:, cs])
    return jnp.concatenate(outs, axis=1)


def _pool_sample_kernel(pb_ref, u_ref, pw_ref, ps_ref, o_ref, *, t_new):
    def row(i):
        return pb_ref[i] if i < POOL_BUF else u_ref[i - POOL_BUF]

    outs = []
    for t in range(t_new):
        e = POOL_BUF + t
        d_groups = []
        for gi, w in enumerate(POOL_WINDOWS):
            cs = slice(gi * POOL_GROUP_DIM, (gi + 1) * POOL_GROUP_DIM)
            lo = max(e + 1 - w, 0)
            acc = row(lo)[:, cs]
            for i in range(lo + 1, e + 1):
                acc = acc + row(i)[:, cs]
            d_groups.append(acc / float(e + 1 - lo) - u_ref[t][:, cs])
        outs.append(_pool_project(d_groups, pw_ref, ps_ref))
    for t in range(t_new):
        o_ref[t] = outs[t]


def _pool_sample(pbuf_t, u_t, pool_w_bf, pool_scale, layer, t_new):
    batch = u_t.shape[1]
    return pl.pallas_call(
        functools.partial(_pool_sample_kernel, t_new=t_new),
        grid=(1,),
        in_specs=[
            pl.BlockSpec((POOL_BUF, batch, POOL_WIDTH), lambda i: (0, 0, 0)),
            pl.BlockSpec((t_new, batch, POOL_WIDTH), lambda i: (0, 0, 0)),
            pl.BlockSpec((None, len(POOL_WINDOWS), POOL_GROUP_DIM, POOL_GROUP_DIM), lambda i: (layer, 0, 0, 0)),
            pl.BlockSpec((None, 1, POOL_WIDTH), lambda i: (layer, 0, 0)),
        ],
        out_specs=pl.BlockSpec((t_new, batch, POOL_WIDTH), lambda i: (0, 0, 0)),
        out_shape=jax.ShapeDtypeStruct((t_new, batch, POOL_WIDTH), F32),
        compiler_params=_cparams("arbitrary"),
        name="pool_sample",
    )(pbuf_t, u_t, pool_w_bf, pool_scale)


def _merge_kernel(x_ref, oa_ref, ob_ref, oc_ref, zg_ref, wb_ref, wo_ref, o_ref):
    m = None
    for n, br in enumerate((oa_ref, ob_ref, oc_ref)):
        proj = _dot(br[...].astype(BF16), wb_ref[n])
        gate = _sigmoid(zg_ref[:, n * D_MODEL:(n + 1) * D_MODEL])
        m = gate * proj if m is None else m + gate * proj
    o_ref[...] = x_ref[...] + _dot(m.astype(BF16), wo_ref[...])


def _merge(x, oa, ob, oc, zg, w_branch_bf, w_o_bf, layer, tm):
    m = x.shape[0]
    row = lambda i: (i, 0)
    return pl.pallas_call(
        _merge_kernel,
        grid=(m // tm,),
        in_specs=[
            pl.BlockSpec((tm, D_MODEL), row),
            pl.BlockSpec((tm, BRANCH_WIDTH), row),
            pl.BlockSpec((tm, BRANCH_WIDTH), row),
            pl.BlockSpec((tm, BRANCH_WIDTH), row),
            pl.BlockSpec((tm, N_BRANCH * D_MODEL), row),
            pl.BlockSpec((None, N_BRANCH, BRANCH_WIDTH, D_MODEL), lambda i: (layer, 0, 0, 0)),
            pl.BlockSpec((None, D_MODEL, D_MODEL), lambda i: (layer, 0, 0)),
        ],
        out_specs=pl.BlockSpec((tm, D_MODEL), row),
        out_shape=jax.ShapeDtypeStruct((m, D_MODEL), F32),
        compiler_params=_cparams("arbitrary"),
        name="merge",
    )(x, oa, ob, oc, zg, w_branch_bf, w_o_bf)


_Z_GROUPS = ((0, P3), (P3, P7), (P7, P8)) + tuple(
    (P8 + n * D_MODEL, P8 + (n + 1) * D_MODEL) for n in range(N_BRANCH))


def _project_tile(x, nw, w_ref, z_ref):
    xn = _rms_rows(x, nw).astype(BF16)
    for a, b in _Z_GROUPS:
        z_ref[:, a:b] = _dot(xn, w_ref[:, a:b])


def _swa_block(z, cos, sin, qn, kn, bd, bias, sinks_ref, layer, kprep, vprep):
    q = _head_norm_rope(z[:, :SWA_Q], qn, cos, sin, bd) * (SWA_HEAD_DIM ** -0.5)
    k = _head_norm_rope(z[:, SWA_Q:SWA_Q + SWA_KV], kn, cos, sin, bd)
    v = z[:, SWA_Q + SWA_KV:]
    kc = [a.astype(BF16) for a in _kv_pad(k)]
    vc = [a.astype(BF16) for a in _kv_pad(v)]
    outs = []
    for kvh in range(SWA_KV_HEADS):
        a, b = 2 * kvh, 2 * kvh + 1
        keys = jnp.concatenate([kprep[a], kc[a], kprep[b], kc[b]], axis=0)
        vals = jnp.concatenate([vprep[a], vc[a], vprep[b], vc[b]], axis=0)
        qq = jnp.concatenate([q[:, a * 128:(a + 1) * 128], q[:, b * 128:(b + 1) * 128]], axis=0)
        s = _dot_nt(qq.astype(BF16), keys)
        ps = []
        for half in range(2):
            quads = []
            for r in range(2):
                sm = s[r * WINDOW:(r + 1) * WINDOW, half * 2 * WINDOW:(half + 1) * 2 * WINDOW] + bias
                sink = sinks_ref[layer, SWA_GROUP * kvh + 2 * r + half]
                quads.append(_sink_softmax_rows(sm, sink).astype(BF16))
            ps.append(jnp.concatenate(quads, axis=0))
        og = _dot(jnp.concatenate(ps, axis=1), vals)
        outs.extend([og[:WINDOW], og[WINDOW:]])
    for a in range(2 * SWA_KV_HEADS):
        kprep[a] = kc[a]
        vprep[a] = vc[a]
    return jnp.concatenate(outs, axis=1), k, v


def _hgrn_chunk(hq, hf, v, hg, lb, onorm, tri, lvl, s_scr):
    c = HG_CHUNK
    q, g, kg = _hgrn_gates(hq, hf, lb)
    g_hi, g_mid, g_lo = _split3(g)
    G = _dot(tri, g_hi) + _dot(tri, g_mid) + _dot(tri, g_lo)
    t_col = lax.broadcasted_iota(jnp.int32, (c, 1), 0)
    levels = []
    half = HG_DIAG
    while half < c:
        blk = 2 * half
        gb = jnp.concatenate(
            [jnp.broadcast_to(G[p * blk + half - 1:p * blk + half, :], (blk, G.shape[1]))
             for p in range(c // blk)], axis=0)
        second = (t_col & half) != 0
        e = jnp.exp2(jnp.where(second, G - gb, gb - G))
        levels.append(((q * e).astype(BF16), (kg * e).astype(BF16), lvl == len(levels)))
        half = blk
    g_last = G[c - 1:c, :]
    q_in = (q * jnp.exp2(G)).astype(BF16)
    k_out = (kg * jnp.exp2(jnp.minimum(g_last - G, 0.0))).astype(BF16)
    dec_last = jnp.exp2(g_last)
    v_bf = v.astype(BF16)
    nblk = c // HG_DIAG
    row_in_blk = lax.broadcasted_iota(jnp.int32, (nblk, HG_DIAG, HG_DV), 1)
    o_heads = []
    for h in range(HG_HEADS):
        hs = slice(h * HG_DK, (h + 1) * HG_DK)
        s_h = s_scr[h]
        a_off = jnp.zeros((c, c), F32)
        for qt, kt, same in levels:
            a_off = a_off + jnp.where(same, _dot_nt(qt[:, hs], kt[:, hs]), 0.0)
        o_h = _dot(q_in[:, hs], s_h.astype(BF16)) + _dot(a_off.astype(BF16), v_bf[:, hs])
        q3 = q[:, hs].reshape(nblk, HG_DIAG, HG_DK)
        k3 = kg[:, hs].reshape(nblk, HG_DIAG, HG_DK)
        g3 = G[:, hs].reshape(nblk, HG_DIAG, HG_DK)
        v3 = v[:, hs].reshape(nblk, HG_DIAG, HG_DV)
        od = jnp.zeros((nblk, HG_DIAG, HG_DV), F32)
        for s in range(HG_DIAG):
            e = jnp.exp2(jnp.minimum(g3 - g3[:, s:s + 1, :], 0.0))
            a = jnp.sum(q3 * (k3[:, s:s + 1, :] * e), axis=-1, keepdims=True)
            od = od + a * jnp.where(row_in_blk >= s, v3[:, s:s + 1, :], 0.0)
        o_heads.append(o_h + od.reshape(c, HG_DV))
        s_scr[h] = _col_bcast(dec_last[:, hs]) * s_h + _dot_tn(k_out[:, hs], v_bf[:, hs])
    return _hgrn_out(o_heads, onorm, hg)


def _pool_tile(u, carry, pos0, pw_ref, ps_ref):
    tt = u.shape[0]
    x = jnp.concatenate([carry[...], u], axis=0)
    sums = []
    cur = x
    shift = 1
    for gi in range(len(POOL_WINDOWS)):
        cur = cur + pltpu.roll(cur, shift, 0)
        sums.append(cur[POOL_CARRY:, :POOL_GROUP_DIM])
        cur = cur[:, POOL_GROUP_DIM:]
        shift *= 2
    pos = pos0 + lax.broadcasted_iota(jnp.int32, (tt, 1), 0)
    d_groups = []
    for gi, w in enumerate(POOL_WINDOWS):
        cs = slice(gi * POOL_GROUP_DIM, (gi + 1) * POOL_GROUP_DIM)
        cnt = jnp.minimum(w, pos + 1).astype(F32)
        d_groups.append(sums[gi] / cnt - u[:, cs])
    carry[...] = u[tt - POOL_CARRY:, :]
    return _pool_project(d_groups, pw_ref, ps_ref)


def _layer_mix_kernel(sinks_ref, x_ref, xnext_ref, nw_ref, w_ref, cos_ref, sin_ref, qn_ref, kn_ref, bd_ref,
                      bias_ref, lb_ref, on_ref, tri_ref, lvl_ref, pw_ref, ps_ref, wb_ref, wo_ref,
                      x1_ref, kc_ref, vc_ref, s_out_ref, ulast_ref,
                      z_a, z_b, kprep, vprep, s_scr, carry, *, layer, steps_per_seq):
    s = pl.program_id(0)
    n = s % steps_per_seq
    nw = nw_ref[...]

    @pl.when(s == 0)
    def _():
        _project_tile(x_ref[:MIX_TILE, :], nw, w_ref, z_a)

    @pl.when(n == 0)
    def _():
        kprep[...] = jnp.zeros_like(kprep)
        vprep[...] = jnp.zeros_like(vprep)
        s_scr[...] = jnp.zeros_like(s_scr)
        carry[...] = jnp.zeros_like(carry)

    qn = qn_ref[...]
    kn = kn_ref[...]
    bd = bd_ref[...]
    lb = lb_ref[...]
    onorm = on_ref[...]
    tri = tri_ref[...]
    lvl = lvl_ref[...]
    band = bias_ref[1]
    first = bias_ref[jnp.minimum(n, 1)]

    def mix(z_ref, t):
        r0 = t * MIX_TILE
        o_a, o_b = [], []
        for j in range(MIX_TILE // WINDOW):
            rz = slice(j * WINDOW, (j + 1) * WINDOW)
            rx = slice(r0 + j * WINDOW, r0 + (j + 1) * WINDOW)
            ob, k, v = _swa_block(z_ref[rz, 0:P3], cos_ref[rx, :], sin_ref[rx, :], qn, kn, bd,
                                  first if (t == 0 and j == 0) else band, sinks_ref, layer, kprep, vprep)
            o_b.append(ob)
            kc_ref[...] = k
            vc_ref[...] = v
            o_a.append(_hgrn_chunk(z_ref[rz, P3:P4], z_ref[rz, P4:P5], z_ref[rz, P5:P6], z_ref[rz, P6:P7],
                                   lb, onorm, tri, lvl, s_scr))
        u = z_ref[:, P7:P8]
        ulast_ref[...] = u[MIX_TILE - POOL_CARRY:, :]
        o_c = _pool_tile(u, carry, n * (2 * MIX_TILE) + r0, pw_ref, ps_ref)
        m = None
        for nb, br in enumerate((jnp.concatenate(o_a, axis=0), jnp.concatenate(o_b, axis=0), o_c)):
            gate = _sigmoid(z_ref[:, P8 + nb * D_MODEL:P8 + (nb + 1) * D_MODEL])
            term = gate * _dot(br.astype(BF16), wb_ref[nb])
            m = term if m is None else m + term
        x1_ref[r0:r0 + MIX_TILE, :] = x_ref[r0:r0 + MIX_TILE, :] + _dot(m.astype(BF16), wo_ref[...])

    _project_tile(x_ref[MIX_TILE:, :], nw, w_ref, z_b)
    mix(z_a, 0)
    _project_tile(xnext_ref[...], nw, w_ref, z_a)
    mix(z_b, 1)
    for h in range(HG_HEADS):
        s_out_ref[h] = s_scr[h]


def _layer_mix(x, norm_w, w_in_bf, cos, sin, qn, kn, bd, bias, sinks, lower, onorm, tri, lvl, pool_w_bf, pool_scale,
               w_branch_bf, w_o_bf, layer, batch, seq):
    step_rows = 2 * MIX_TILE
    sps = seq // step_rows
    nsteps = batch * sps
    last_tile = batch * seq // MIX_TILE - 1
    const2 = lambda s: (0, 0)
    lay3 = lambda s: (layer, 0, 0)
    per_seq = lambda s: (s // sps, 0, 0)
    single = pl.Buffered(1)
    return pl.pallas_call(
        functools.partial(_layer_mix_kernel, layer=layer, steps_per_seq=sps),
        grid=(nsteps,),
        in_specs=[
            pl.BlockSpec(memory_space=pltpu.SMEM),
            pl.BlockSpec((step_rows, D_MODEL), lambda s: (s, 0)),
            pl.BlockSpec((MIX_TILE, D_MODEL), lambda s: (jnp.minimum(2 * s + 2, last_tile), 0)),
            pl.BlockSpec((None, 1, D_MODEL), lay3),
            pl.BlockSpec((None, D_MODEL, D_IN), lay3, pipeline_mode=single),
            pl.BlockSpec((step_rows, 128), lambda s: (s % sps, 0)),
            pl.BlockSpec((step_rows, 128), lambda s: (s % sps, 0)),
            pl.BlockSpec((1, SWA_Q), const2),
            pl.BlockSpec((1, SWA_KV), const2),
            pl.BlockSpec((256, 256), const2),
            pl.BlockSpec((2, WINDOW, 2 * WINDOW), lambda s: (0, 0, 0)),
            pl.BlockSpec((None, 1, HG_K), lay3),
            pl.BlockSpec((None, 1, HG_DV), lay3),
            pl.BlockSpec((HG_CHUNK, HG_CHUNK), const2),
            pl.BlockSpec((HG_CHUNK, HG_CHUNK), const2),
            pl.BlockSpec((None, len(POOL_WINDOWS), POOL_GROUP_DIM, POOL_GROUP_DIM), lambda s: (layer, 0, 0, 0)),
            pl.BlockSpec((None, 1, POOL_WIDTH), lay3),
            pl.BlockSpec((None, N_BRANCH, BRANCH_WIDTH, D_MODEL), lambda s: (layer, 0, 0, 0), pipeline_mode=single),
            pl.BlockSpec((None, D_MODEL, D_MODEL), lay3, pipeline_mode=single),
        ],
        out_specs=[
            pl.BlockSpec((step_rows, D_MODEL), lambda s: (s, 0)),
            pl.BlockSpec((None, WINDOW, SWA_KV), per_seq),
            pl.BlockSpec((None, WINDOW, SWA_KV), per_seq),
            pl.BlockSpec((None, HG_HEADS, HG_DK, HG_DV), lambda s: (s // sps, 0, 0, 0)),
            pl.BlockSpec((None, POOL_CARRY, POOL_WIDTH), per_seq),
        ],
        out_shape=[
            jax.ShapeDtypeStruct((batch * seq, D_MODEL), F32),
            jax.ShapeDtypeStruct((batch, WINDOW, SWA_KV), F32),
            jax.ShapeDtypeStruct((batch, WINDOW, SWA_KV), F32),
            jax.ShapeDtypeStruct((batch, HG_HEADS, HG_DK, HG_DV), F32),
            jax.ShapeDtypeStruct((batch, POOL_CARRY, POOL_WIDTH), F32),
        ],
        scratch_shapes=[
            pltpu.VMEM((MIX_TILE, D_IN), F32),
            pltpu.VMEM((MIX_TILE, D_IN), F32),
            pltpu.VMEM((2 * SWA_KV_HEADS, WINDOW, 128), BF16),
            pltpu.VMEM((2 * SWA_KV_HEADS, WINDOW, 128), BF16),
            pltpu.VMEM((HG_HEADS, HG_DK, HG_DV), F32),
            pltpu.VMEM((POOL_CARRY, POOL_WIDTH), F32),
        ],
        compiler_params=_cparams("arbitrary"),
        name="layer_mix",
    )(sinks, x, x, norm_w, w_in_bf, cos, sin, qn, kn, bd, bias, lower, onorm, tri, lvl, pool_w_bf, pool_scale,
      w_branch_bf, w_o_bf)


def _ffn_kernel(x_ref, nw_ref, wu_ref, wd_ref, o_ref):
    x = x_ref[...]
    xn = _rms_rows(x, nw_ref[...]).astype(BF16)
    acc = x
    step = D_MODEL
    for c in range(0, D_FF, step):
        h = _dot(xn, wu_ref[:, c:c + step])
        h = jnp.square(jnp.maximum(h, 0.0)).astype(BF16)
        acc = acc + _dot(h, wd_ref[c:c + step, :])
    o_ref[...] = acc


def _ffn(x, norm_w, w_up_bf, w_down_bf, layer, tm):
    m = x.shape[0]
    row = lambda i: (i, 0)
    return pl.pallas_call(
        _ffn_kernel,
        grid=(m // tm,),
        in_specs=[
            pl.BlockSpec((tm, D_MODEL), row),
            pl.BlockSpec((None, 1, D_MODEL), lambda i: (layer, 0, 0)),
            pl.BlockSpec((None, D_MODEL, D_FF), lambda i: (layer, 0, 0), pipeline_mode=pl.Buffered(1)),
            pl.BlockSpec((None, D_FF, D_MODEL), lambda i: (layer, 0, 0), pipeline_mode=pl.Buffered(1)),
        ],
        out_specs=pl.BlockSpec((tm, D_MODEL), row),
        out_shape=jax.ShapeDtypeStruct((m, D_MODEL), F32),
        compiler_params=_cparams("arbitrary"),
        name="ffn",
    )(x, norm_w, w_up_bf, w_down_bf)


def _rope_tables(pos):
    hd = SWA_HEAD_DIM
    inv = jnp.power(ROPE_THETA, -jnp.arange(0, hd, 2, dtype=F32) / hd)
    ang = pos[:, None] * inv[None, :]
    cos = jnp.cos(ang)
    sin = jnp.sin(ang)
    cos_t = jnp.concatenate([cos, cos, cos, cos], axis=1)
    sin_t = jnp.concatenate([-sin, sin, -sin, sin], axis=1)
    return cos_t, sin_t


def _hgrn_levels():
    t = jnp.arange(HG_CHUNK)[:, None]
    s = jnp.arange(HG_CHUNK)[None, :]
    lvl = jnp.full((HG_CHUNK, HG_CHUNK), -1, jnp.int32)
    half, level = HG_DIAG, 0
    while half < HG_CHUNK:
        shift = (2 * half).bit_length() - 1
        own = ((t >> shift) == (s >> shift)) & ((t & half) != 0) & ((s & half) == 0)
        lvl = jnp.where(own, level, lvl)
        half, level = 2 * half, level + 1
    return lvl


def _prompt_bias():
    i = jnp.arange(WINDOW)[:, None]
    j = jnp.arange(2 * WINDOW)[None, :]
    band = (j > i) & (j <= i + WINDOW)
    first = band & (j >= WINDOW)
    return jnp.where(jnp.stack([first, band]), 0.0, NEG).astype(F32)


def _sample_bias(wb, t_new):
    grp = 8 // t_new
    n_cache = grp * wb
    nkeys = -(-(n_cache + 8) // 128) * 128
    r = jnp.arange(8)[:, None]
    seq_r, t = r // t_new, r % t_new
    j = jnp.arange(nkeys)[None, :]
    cache_ok = (j < n_cache) & (j // wb == seq_r) & (wb + t - j % wb < WINDOW)
    jn = j - n_cache
    new_ok = (jn >= 0) & (jn < 8) & (jn // t_new == seq_r) & (jn % t_new <= t)
    return jnp.where(cache_ok | new_ok, 0.0, NEG).astype(F32)


def _row_tile(m, cap):
    t = min(m, cap)
    while m % t:
        t //= 2
    return t


def kernel(x_prompt, x_sample, state_hgrn, cache_swa_k, cache_swa_v, state_pool, norm_mix, w_in, q_norm, k_norm,
           attn_sinks, hgrn_lb, hgrn_onorm, pool_w, pool_scale, w_branch, w_o, norm_ffn, w_up, w_down):
    depth = w_in.shape[0]
    bp, seq, _ = x_prompt.shape
    bs, t_new, _ = x_sample.shape
    wb = cache_swa_k.shape[2]

    w_in_bf = w_in.astype(BF16)
    w_branch_bf = w_branch.astype(BF16)
    w_o_bf = w_o.astype(BF16)
    w_up_bf = w_up.astype(BF16)
    w_down_bf = w_down.astype(BF16)
    pool_w_bf = pool_w.astype(BF16)
    norm_mix3 = norm_mix.reshape(depth, 1, D_MODEL)
    norm_ffn3 = norm_ffn.reshape(depth, 1, D_MODEL)
    onorm3 = hgrn_onorm.reshape(depth, 1, HG_DV)
    pool_scale3 = pool_scale.reshape(depth, 1, POOL_WIDTH)
    lower3 = _lower_bounds(hgrn_lb).reshape(depth, 1, HG_K)
    cache_k4 = cache_swa_k.reshape(depth, bs, wb, SWA_KV)
    cache_v4 = cache_swa_v.reshape(depth, bs, wb, SWA_KV)

    cos_p, sin_p = _rope_tables(jnp.arange(seq, dtype=F32))
    cos_s, sin_s = _rope_tables(jnp.arange(t_new, dtype=F32) + PAST_LEN)
    bb_swa = SWA_SAMPLE_BB if bs % SWA_SAMPLE_BB == 0 else bs
    cos_s = jnp.tile(cos_s, (bb_swa, 1))
    sin_s = jnp.tile(sin_s, (bb_swa, 1))
    bias_p = _prompt_bias()
    bias_s = _sample_bias(wb, t_new)
    lane = jnp.arange(256)
    bd = (lane[:, None] // SWA_HEAD_DIM == lane[None, :] // SWA_HEAD_DIM).astype(BF16)
    tok = jnp.arange(HG_CHUNK)
    tri = (tok[:, None] >= tok[None, :]).astype(BF16)
    lvl = _hgrn_levels()

    hp = x_prompt.reshape(bp * seq, D_MODEL)
    hs = x_sample.reshape(bs * t_new, D_MODEL)
    tm_p = _row_tile(bp * seq, 512)
    tm_s = _row_tile(bs * t_new, 512)

    sp = ([], [], [], [])
    ss = ([], [], [], [])
    kv_stacks = (cache_k4, cache_v4)
    for l in range(depth):
        qn = jnp.tile(q_norm[l], SWA_HEADS).reshape(1, SWA_Q)
        kn = jnp.tile(k_norm[l], SWA_KV_HEADS).reshape(1, SWA_KV)

        x1, k_last, v_last, s_new, u_last = _layer_mix(
            hp, norm_mix3, w_in_bf, cos_p, sin_p, qn, kn, bd, bias_p, attn_sinks, lower3, onorm3, tri, lvl,
            pool_w_bf, pool_scale3, w_branch_bf, w_o_bf, l, bp, seq)
        hp = _ffn(x1, norm_ffn3, w_up_bf, w_down_bf, l, tm_p)
        sp[0].append(s_new)
        sp[1].append(k_last.reshape(bp, WINDOW, SWA_KV_HEADS, SWA_HEAD_DIM))
        sp[2].append(v_last.reshape(bp, WINDOW, SWA_KV_HEADS, SWA_HEAD_DIM))
        sp[3].append(u_last[:, POOL_CARRY - POOL_BUF:])

        qkv, hq, hf, hi, hg, u, zg = _proj_in(hs, norm_mix3, w_in_bf, l, tm_s)
        o_b, k_stack, v_stack = _swa_sample(qkv, cos_s, sin_s, qn, kn, bd, bias_s, attn_sinks, cache_k4, cache_v4,
                                            kv_stacks, l, bs, t_new)
        kv_stacks = (k_stack, v_stack)
        o_a, s_new = _hgrn_sample(hq, hf, hi, hg, lower3, onorm3, state_hgrn, l, bs, t_new)
        ss[0].append(s_new)
        u3 = u.reshape(bs, t_new, POOL_WIDTH)
        o_c = _pool_sample(jnp.swapaxes(state_pool[l], 0, 1), jnp.swapaxes(u3, 0, 1),
                           pool_w_bf, pool_scale3, l, t_new)
        o_c = jnp.swapaxes(o_c, 0, 1).reshape(bs * t_new, POOL_WIDTH)
        x1 = _merge(hs, o_a, o_b, o_c, zg, w_branch_bf, w_o_bf, l, tm_s)
        hs = _ffn(x1, norm_ffn3, w_up_bf, w_down_bf, l, tm_s)
        ss[3].append(jnp.concatenate([state_pool[l], u3], axis=1)[:, -POOL_BUF:])

    cache_shape = (depth, bs, wb, SWA_KV_HEADS, SWA_HEAD_DIM)
    return (hp.reshape(bp, seq, D_MODEL), hs.reshape(bs, t_new, D_MODEL),
            jnp.stack(sp[0]), jnp.stack(sp[1]), jnp.stack(sp[2]), jnp.stack(sp[3]),
            jnp.stack(ss[0]), kv_stacks[0].reshape(cache_shape), kv_stacks[1].reshape(cache_shape),
            jnp.stack(ss[3]))
```

```python
import functools

import jax
import jax.numpy as jnp
from jax import lax
from jax.experimental import pallas as pl
from jax.experimental.pallas import tpu as pltpu

F32 = jnp.float32
BF16 = jnp.bfloat16

D_MODEL = 1024
PAST_LEN = 8192
SWA_HEADS = 8
SWA_KV_HEADS = 2
SWA_HEAD_DIM = 64
SWA_GROUP = SWA_HEADS // SWA_KV_HEADS
WINDOW = 128
ROPE_THETA = 10000.0
HG_HEADS = 4
HG_DK = 128
HG_DV = 128
LB_FLOOR = 1e-30
POOL_WINDOWS = (2, 4, 8, 16)
POOL_GROUP_DIM = 128
POOL_WIDTH = 512
POOL_BUF = 15
BRANCH_WIDTH = 512
N_BRANCH = 3
D_FF = 4 * D_MODEL
EPS = 1e-6
NEG = -1e30
LOG2E = 1.4426950408889634

SWA_Q = SWA_HEADS * SWA_HEAD_DIM
SWA_KV = SWA_KV_HEADS * SWA_HEAD_DIM
HG_K = HG_HEADS * HG_DK
QKV_W = SWA_Q + 2 * SWA_KV
P3 = QKV_W
P4 = P3 + HG_K
P5 = P4 + HG_K
P6 = P5 + HG_K
P7 = P6 + HG_K
P8 = P7 + POOL_WIDTH
D_IN = P8 + N_BRANCH * D_MODEL

VMEM_LIMIT_BYTES = 56 * 1024 * 1024
HG_CHUNK = 128
HG_DIAG = 8
POOL_CARRY = 16
SAMPLE_BB = 8
MIX_ROWS = 512
SWA_SAMPLE_BB = 16


def _cparams(*sem):
    return pltpu.CompilerParams(dimension_semantics=sem, vmem_limit_bytes=VMEM_LIMIT_BYTES)


def _rms_rows(x, w):
    ms = jnp.mean(x * x, axis=-1, keepdims=True)
    return x * lax.rsqrt(ms + EPS) * w


def _sigmoid(z):
    return 1.0 / (1.0 + jnp.exp2(z * (-LOG2E)))


def _silu(z):
    return z * _sigmoid(z)


def _split3(x):
    hi = x.astype(BF16)
    r1 = x - hi.astype(F32)
    mid = r1.astype(BF16)
    lo = (r1 - mid.astype(F32)).astype(BF16)
    return hi, mid, lo


def _dot(a, b):
    return jnp.dot(a, b, preferred_element_type=F32)


def _dot_nt(a, b):
    return lax.dot_general(a, b, (((1,), (1,)), ((), ())), preferred_element_type=F32)


def _dot_tn(a, b):
    return lax.dot_general(a, b, (((0,), (0,)), ((), ())), preferred_element_type=F32)


def _lower_kernel(lb_ref, o_ref):
    x = lb_ref[...]
    depth = x.shape[0]
    rows = [x[i:i + 1] for i in range(depth)]
    m = rows[0]
    for r in rows[1:]:
        m = jnp.maximum(m, r)
    es = [jnp.exp(r - m) for r in rows]
    tot = es[0]
    for e in es[1:]:
        tot = tot + e
    ps = [e / tot for e in es]
    out = []
    c = None
    for p in ps:
        c = p if c is None else c + p
        out.append(jnp.maximum(c - ps[0], 0.0))
    o_ref[...] = jnp.concatenate(out, axis=0)


def _lower_bounds(hgrn_lb):
    return pl.pallas_call(
        _lower_kernel,
        out_shape=jax.ShapeDtypeStruct(hgrn_lb.shape, F32),
        name="lower_bounds",
    )(hgrn_lb)


def _proj_in_kernel(x_ref, nw_ref, w_ref, qkv_ref, hq_ref, hf_ref, hi_ref, hg_ref, u_ref, zg_ref):
    xn = _rms_rows(x_ref[...], nw_ref[...]).astype(BF16)

    def mm(a, b):
        return _dot(xn, w_ref[:, a:b])

    qkv_ref[...] = mm(0, P3)
    hq_ref[...] = mm(P3, P4)
    hf_ref[...] = mm(P4, P5)
    hi_ref[...] = mm(P5, P6)
    hg_ref[...] = mm(P6, P7)
    u_ref[...] = mm(P7, P8)
    for n in range(N_BRANCH):
        zg_ref[:, n * D_MODEL:(n + 1) * D_MODEL] = mm(P8 + n * D_MODEL, P8 + (n + 1) * D_MODEL)


def _proj_in(x, norm_w, w_in_bf, layer, tm):
    m = x.shape[0]
    widths = (QKV_W, HG_K, HG_K, HG_K, HG_K, POOL_WIDTH, N_BRANCH * D_MODEL)
    return pl.pallas_call(
        _proj_in_kernel,
        grid=(m // tm,),
        in_specs=[
            pl.BlockSpec((tm, D_MODEL), lambda i: (i, 0)),
            pl.BlockSpec((None, 1, D_MODEL), lambda i: (layer, 0, 0)),
            pl.BlockSpec((None, D_MODEL, D_IN), lambda i: (layer, 0, 0), pipeline_mode=pl.Buffered(1)),
        ],
        out_specs=[pl.BlockSpec((tm, w), lambda i: (i, 0)) for w in widths],
        out_shape=[jax.ShapeDtypeStruct((m, w), F32) for w in widths],
        compiler_params=_cparams("arbitrary"),
        name="proj_in",
    )(x, norm_w, w_in_bf)


def _head_norm_rope(x, w_row, cos, sin, bd):
    width = x.shape[1]
    sq = x * x
    hi = sq.astype(BF16)
    lo = (sq - hi.astype(F32)).astype(BF16)
    parts = []
    for c in range(0, width, 256):
        wc = min(256, width - c)
        b = bd[:wc, :wc]
        parts.append(_dot(hi[:, c:c + wc], b) + _dot(lo[:, c:c + wc], b))
    ms = (parts[0] if len(parts) == 1 else jnp.concatenate(parts, axis=1)) * (1.0 / SWA_HEAD_DIM)
    y = x * lax.rsqrt(ms + EPS) * w_row
    reps = width // 128
    cos_t = cos if reps == 1 else jnp.concatenate([cos] * reps, axis=1)
    sin_t = sin if reps == 1 else jnp.concatenate([sin] * reps, axis=1)
    lane = lax.broadcasted_iota(jnp.int32, y.shape, 1)
    first_half = (lane & (SWA_HEAD_DIM - 1)) < (SWA_HEAD_DIM // 2)
    half = SWA_HEAD_DIM // 2
    rot = jnp.where(first_half, pltpu.roll(y, width - half, 1), pltpu.roll(y, half, 1))
    return y * cos_t + rot * sin_t


def _kv_pad(x):
    lo = lax.broadcasted_iota(jnp.int32, x.shape, 1) < SWA_HEAD_DIM
    xr = pltpu.roll(x, SWA_HEAD_DIM, 1)
    return (jnp.where(lo, x, 0.0), jnp.where(lo, 0.0, xr), jnp.where(lo, xr, 0.0), jnp.where(lo, 0.0, x))


def _sink_softmax_rows(sm, sink):
    mx = jnp.maximum(jnp.max(sm, axis=-1, keepdims=True), sink)
    e = jnp.exp(sm - mx)
    den = jnp.sum(e, axis=-1, keepdims=True) + jnp.exp(sink - mx)
    return e * (1.0 / den)


def _swa_sample_kernel(sinks_ref, qkv_ref, cos_ref, sin_ref, qn_ref, kn_ref, bd_ref, bias_ref, kb_ref, vb_ref,
                       kstack_ref, vstack_ref, o_ref, kc_out_ref, vc_out_ref, *, layer, t_new):
    del kstack_ref, vstack_ref
    z = qkv_ref[...]
    cos = cos_ref[...]
    sin = sin_ref[...]
    bd = bd_ref[...]
    q = _head_norm_rope(z[:, :SWA_Q], qn_ref[...], cos, sin, bd) * (SWA_HEAD_DIM ** -0.5)
    k = _head_norm_rope(z[:, SWA_Q:SWA_Q + SWA_KV], kn_ref[...], cos, sin, bd)
    v = z[:, SWA_Q + SWA_KV:]

    rows = z.shape[0]
    ng = rows // 8
    nkeys = bias_ref.shape[1]
    nseq, wb = kb_ref.shape[0], kb_ref.shape[1]
    for cache_ref, out_ref, new in ((kb_ref, kc_out_ref, k), (vb_ref, vc_out_ref, v)):
        flat = cache_ref[...].reshape(nseq * wb, SWA_KV)
        out_ref[...] = pltpu.roll(flat, nseq * wb - t_new, 0).reshape(nseq, wb, SWA_KV)
        for b in range(nseq):
            out_ref[b, wb - t_new:wb, :] = new[b * t_new:(b + 1) * t_new]
    lo = lax.broadcasted_iota(jnp.int32, (rows, 128), 1) < SWA_HEAD_DIM
    qw = []
    for h in range(SWA_HEADS):
        qc = q[:, (h // 2) * 128:(h // 2 + 1) * 128]
        want_hi = h // SWA_GROUP == 1
        if (h % 2 == 1) != want_hi:
            qc = pltpu.roll(qc, SWA_HEAD_DIM, 1)
        qw.append((jnp.where(lo, 0.0, qc) if want_hi else jnp.where(lo, qc, 0.0)).reshape(ng, 8, 128))
    qw = jnp.concatenate(qw, axis=1).astype(BF16)

    def keys_of(cache_ref, new):
        cache = cache_ref[...].reshape(ng, (8 // t_new) * wb, SWA_KV)
        pad = jnp.zeros((ng, nkeys - cache.shape[1] - 8, SWA_KV), F32)
        return jnp.concatenate([cache, new.reshape(ng, 8, SWA_KV), pad], axis=1).astype(BF16)

    s = jnp.einsum('gqd,gkd->gqk', qw, keys_of(kb_ref, k), preferred_element_type=F32)
    bias = bias_ref[...]
    p = jnp.concatenate(
        [_sink_softmax_rows(s[:, h * 8:(h + 1) * 8, :] + bias, sinks_ref[layer, h]).astype(BF16)
         for h in range(SWA_HEADS)], axis=1)
    res = jnp.einsum('gqk,gkd->gqd', p, keys_of(vb_ref, v), preferred_element_type=F32)
    res = res.reshape(ng * SWA_HEADS * 8, 128)
    res_r = pltpu.roll(res, SWA_HEAD_DIM, 1)
    res = res.reshape(ng, SWA_HEADS * 8, 128)
    res_r = res_r.reshape(ng, SWA_HEADS * 8, 128)
    lo3 = lo.reshape(ng, 8, 128)
    cols = []
    for c in range(SWA_HEADS // 2):
        kv_hi = (2 * c) // SWA_GROUP == 1
        even = (res_r if kv_hi else res)[:, (2 * c) * 8:(2 * c + 1) * 8, :]
        odd = (res if kv_hi else res_r)[:, (2 * c + 1) * 8:(2 * c + 2) * 8, :]
        cols.append(jnp.where(lo3, even, odd).reshape(rows, 128))
    o_ref[...] = jnp.concatenate(cols, axis=1)


def _swa_sample(qkv, cos, sin, qn, kn, bd, bias, sinks, kbuf, vbuf, stacked_prev, layer, batch, t_new):
    bb = SWA_SAMPLE_BB if batch % SWA_SAMPLE_BB == 0 else batch
    rb = bb * t_new
    wb = kbuf.shape[2]
    row = lambda i: (i, 0)
    const = lambda i: (0, 0)
    cache_blk = pl.BlockSpec((None, bb, wb, SWA_KV), lambda i: (layer, i, 0, 0))
    any_spec = pl.BlockSpec(memory_space=pl.ANY)
    return pl.pallas_call(
        functools.partial(_swa_sample_kernel, layer=layer, t_new=t_new),
        grid=(batch // bb,),
        in_specs=[
            pl.BlockSpec(memory_space=pltpu.SMEM),
            pl.BlockSpec((rb, QKV_W), row),
            pl.BlockSpec((rb, 128), const),
            pl.BlockSpec((rb, 128), const),
            pl.BlockSpec((1, SWA_Q), const),
            pl.BlockSpec((1, SWA_KV), const),
            pl.BlockSpec((256, 256), const),
            pl.BlockSpec(bias.shape, const),
            cache_blk,
            cache_blk,
            any_spec,
            any_spec,
        ],
        out_specs=[pl.BlockSpec((rb, SWA_Q), row), cache_blk, cache_blk],
        out_shape=[
            jax.ShapeDtypeStruct((batch * t_new, SWA_Q), F32),
            jax.ShapeDtypeStruct(kbuf.shape, F32),
            jax.ShapeDtypeStruct(vbuf.shape, F32),
        ],
        input_output_aliases={10: 1, 11: 2},
        compiler_params=_cparams("arbitrary"),
        name="swa_sample",
    )(sinks, qkv, cos, sin, qn, kn, bd, bias, kbuf, vbuf, stacked_prev[0], stacked_prev[1])


def _hgrn_gates(hq, hf, lb):
    sig_p = _sigmoid(hf)
    om = 1.0 - lb
    log2f = jnp.log(jnp.maximum(lb, LB_FLOOR) + om * sig_p) * LOG2E
    return _silu(hq), log2f, om * (1.0 - sig_p)


def _hgrn_out(o, onorm, hg):
    outs = []
    for h in range(HG_HEADS):
        hs = slice(h * HG_DV, (h + 1) * HG_DV)
        outs.append(_rms_rows(o[h], onorm) * _silu(hg[:, hs]))
    return jnp.concatenate(outs, axis=1)


def _col_bcast(row):
    return jnp.transpose(jnp.broadcast_to(row, (row.shape[1], row.shape[1])))


def _hgrn_sample_kernel(hq_ref, hf_ref, hi_ref, hg_ref, lb_ref, on_ref, s0_ref, o_ref, s_out_ref, *, t_new):
    q, g, kg = _hgrn_gates(hq_ref[...], hf_ref[...], lb_ref[...])
    v = hi_ref[...]
    rows = q.shape[0]
    t_idx = lax.broadcasted_iota(jnp.int32, (rows, 1), 0) % t_new

    def down(x, d):
        return pltpu.roll(x, d, 0)

    def up(x, d):
        return pltpu.roll(x, rows - d, 0)

    G = g
    for d in range(1, t_new):
        G = G + jnp.where(t_idx >= d, down(g, d), 0.0)
    rest = jnp.zeros_like(g)
    for d in range(1, t_new):
        rest = rest + jnp.where(t_idx + d < t_new, up(g, d), 0.0)
    q_in = q * jnp.exp2(G)
    k_out = kg * jnp.exp2(rest)
    dec_all = jnp.exp2(G + rest)

    o_intra = [jnp.zeros((rows, HG_DV), F32) for _ in range(HG_HEADS)]
    for d in range(t_new):
        kd = kg if d == 0 else down(kg, d)
        vd = v if d == 0 else down(v, d)
        gd = G if d == 0 else down(G, d)
        w = q * kd * jnp.exp2(jnp.minimum(G - gd, 0.0))
        for h in range(HG_HEADS):
            hs = slice(h * HG_DK, (h + 1) * HG_DK)
            a = jnp.sum(w[:, hs], axis=-1, keepdims=True)
            o_intra[h] = o_intra[h] + jnp.where(t_idx >= d, a, 0.0) * vd[:, hs]

    dec_hi = dec_all.astype(BF16).astype(F32)
    dec_r = dec_all - dec_hi
    dec_mid = dec_r.astype(BF16).astype(F32)
    dec_lo = (dec_r - dec_mid).astype(BF16).astype(F32)
    dec_parts = jnp.where(t_idx == 0, dec_hi, jnp.where(t_idx == 1, dec_mid, jnp.where(t_idx == 2, dec_lo, 0.0)))
    seq_of_row = lax.broadcasted_iota(jnp.int32, (rows, 1), 0) // t_new
    ones = jnp.ones((rows, HG_DV), BF16)
    v_bf = v.astype(BF16)
    o = []
    for h in range(HG_HEADS):
        hs = slice(h * HG_DK, (h + 1) * HG_DK)
        o_h = o_intra[h]
        for b in range(s0_ref.shape[0]):
            mine = seq_of_row == b
            s_h = s0_ref[b, h]
            o_h = o_h + _dot(jnp.where(mine, q_in[:, hs], 0.0).astype(BF16), s_h.astype(BF16))
            dec_mat = _dot_tn(jnp.where(mine, dec_parts[:, hs], 0.0).astype(BF16), ones)
            upd = _dot_tn(jnp.where(mine, k_out[:, hs], 0.0).astype(BF16), v_bf[:, hs])
            s_out_ref[b, h] = dec_mat * s_h + upd
        o.append(o_h)
    o_ref[...] = _hgrn_out(o, on_ref[...], hg_ref[...])


def _hgrn_sample(hq, hf, hi, hg, lower, onorm, state, layer, batch, t_new):
    assert t_new >= 3
    rb = SAMPLE_BB * t_new
    blk = pl.BlockSpec((rb, HG_K), lambda i: (i, 0))
    return pl.pallas_call(
        functools.partial(_hgrn_sample_kernel, t_new=t_new),
        grid=(batch // SAMPLE_BB,),
        in_specs=[
            blk, blk, blk, blk,
            pl.BlockSpec((None, 1, HG_K), lambda i: (layer, 0, 0)),
            pl.BlockSpec((None, 1, HG_DV), lambda i: (layer, 0, 0)),
            pl.BlockSpec((None, SAMPLE_BB, HG_HEADS, HG_DK, HG_DV), lambda i: (layer, i, 0, 0, 0)),
        ],
        out_specs=[blk, pl.BlockSpec((SAMPLE_BB, HG_HEADS, HG_DK, HG_DV), lambda i: (i, 0, 0, 0))],
        out_shape=[
            jax.ShapeDtypeStruct((batch * t_new, HG_K), F32),
            jax.ShapeDtypeStruct((batch, HG_HEADS, HG_DK, HG_DV), state.dtype),
        ],
        compiler_params=_cparams("arbitrary"),
        name="hgrn_sample",
    )(hq, hf, hi, hg, lower, onorm, state)


def _pool_project(d_groups, pw_ref, ps_ref):
    outs = []
    for gi, d in enumerate(d_groups):
        cs = slice(gi * POOL_GROUP_DIM, (gi + 1) * POOL_GROUP_DIM)
        outs.append(_dot(d.astype(BF16), pw_ref[gi]) * ps_ref[:, cs])
    return jnp.concatenate(outs, axis=1)


def _pool_sample_kernel(pb_ref, u_ref, pw_ref, ps_ref, o_ref, *, t_new):
    def row(i):
        return pb_ref[i] if i < POOL_BUF else u_ref[i - POOL_BUF]

    outs = []
    for t in range(t_new):
        e = POOL_BUF + t
        d_groups = []
        for gi, w in enumerate(POOL_WINDOWS):
            cs = slice(gi * POOL_GROUP_DIM, (gi + 1) * POOL_GROUP_DIM)
            lo = max(e + 1 - w, 0)
            acc = row(lo)[:, cs]
            for i in range(lo + 1, e + 1):
                acc = acc + row(i)[:, cs]
            d_groups.append(acc / float(e + 1 - lo) - u_ref[t][:, cs])
        outs.append(_pool_project(d_groups, pw_ref, ps_ref))
    for t in range(t_new):
        o_ref[t] = outs[t]


def _pool_sample(pbuf_t, u_t, pool_w_bf, pool_scale, layer, t_new):
    batch = u_t.shape[1]
    return pl.pallas_call(
        functools.partial(_pool_sample_kernel, t_new=t_new),
        grid=(1,),
        in_specs=[
            pl.BlockSpec((POOL_BUF, batch, POOL_WIDTH), lambda i: (0, 0, 0)),
            pl.BlockSpec((t_new, batch, POOL_WIDTH), lambda i: (0, 0, 0)),
            pl.BlockSpec((None, len(POOL_WINDOWS), POOL_GROUP_DIM, POOL_GROUP_DIM), lambda i: (layer, 0, 0, 0)),
            pl.BlockSpec((None, 1, POOL_WIDTH), lambda i: (layer, 0, 0)),
        ],
        out_specs=pl.BlockSpec((t_new, batch, POOL_WIDTH), lambda i: (0, 0, 0)),
        out_shape=jax.ShapeDtypeStruct((t_new, batch, POOL_WIDTH), F32),
        compiler_params=_cparams("arbitrary"),
        name="pool_sample",
    )(pbuf_t, u_t, pool_w_bf, pool_scale)


def _merge_kernel(x_ref, oa_ref, ob_ref, oc_ref, zg_ref, wb_ref, wo_ref, o_ref):
    m = None
    for n, br in enumerate((oa_ref, ob_ref, oc_ref)):
        proj = _dot(br[...].astype(BF16), wb_ref[n])
        gate = _sigmoid(zg_ref[:, n * D_MODEL:(n + 1) * D_MODEL])
        m = gate * proj if m is None else m + gate * proj
    o_ref[...] = x_ref[...] + _dot(m.astype(BF16), wo_ref[...])


def _merge(x, oa, ob, oc, zg, w_branch_bf, w_o_bf, layer, tm):
    m = x.shape[0]
    row = lambda i: (i, 0)
    return pl.pallas_call(
        _merge_kernel,
        grid=(m // tm,),
        in_specs=[
            pl.BlockSpec((tm, D_MODEL), row),
            pl.BlockSpec((tm, BRANCH_WIDTH), row),
            pl.BlockSpec((tm, BRANCH_WIDTH), row),
            pl.BlockSpec((tm, BRANCH_WIDTH), row),
            pl.BlockSpec((tm, N_BRANCH * D_MODEL), row),
            pl.BlockSpec((None, N_BRANCH, BRANCH_WIDTH, D_MODEL), lambda i: (layer, 0, 0, 0)),
            pl.BlockSpec((None, D_MODEL, D_MODEL), lambda i: (layer, 0, 0)),
        ],
        out_specs=pl.BlockSpec((tm, D_MODEL), row),
        out_shape=jax.ShapeDtypeStruct((m, D_MODEL), F32),
        compiler_params=_cparams("arbitrary"),
        name="merge",
    )(x, oa, ob, oc, zg, w_branch_bf, w_o_bf)


_Z_GROUPS = ((0, P3), (P3, P7), (P7, P8)) + tuple(
    (P8 + n * D_MODEL, P8 + (n + 1) * D_MODEL) for n in range(N_BRANCH))


def _project_tile(x, nw, w_ref, z_ref):
    xn = _rms_rows(x, nw).astype(BF16)
    for a, b in _Z_GROUPS:
        z_ref[:, a:b] = _dot(xn, w_ref[:, a:b])


def _swa_blocks(z, cos, sin, qn, kn, bd, biases, sinks_ref, layer, kprep, vprep):
    nblk = len(biases)
    q = _head_norm_rope(z[:, :SWA_Q], qn, cos, sin, bd) * (SWA_HEAD_DIM ** -0.5)
    k = _head_norm_rope(z[:, SWA_Q:SWA_Q + SWA_KV], kn, cos, sin, bd)
    v = z[:, SWA_Q + SWA_KV:]
    kc = [a.astype(BF16) for a in _kv_pad(k)]
    vc = [a.astype(BF16) for a in _kv_pad(v)]

    def blk(x, j):
        return x[j * WINDOW:(j + 1) * WINDOW]

    def prev_cur(prep, cur, pad, j):
        return (prep[pad] if j == 0 else blk(cur[pad], j - 1)), blk(cur[pad], j)

    bias_all = jnp.concatenate(biases, axis=0)
    out_cols = [[None] * nblk for _ in range(SWA_HEADS // 2)]
    for kvh in range(SWA_KV_HEADS):
        a, b = 2 * kvh, 2 * kvh + 1
        scores = []
        for j in range(nblk):
            keys = jnp.concatenate(prev_cur(kprep, kc, a, j) + prev_cur(kprep, kc, b, j), axis=0)
            qq = jnp.concatenate([blk(q[:, a * 128:(a + 1) * 128], j), blk(q[:, b * 128:(b + 1) * 128], j)], axis=0)
            scores.append(_dot_nt(qq.astype(BF16), keys))
        probs = [[[None, None] for _ in range(2)] for _ in range(nblk)]
        for half in range(2):
            for r in range(2):
                sm = jnp.concatenate(
                    [s[r * WINDOW:(r + 1) * WINDOW, half * 2 * WINDOW:(half + 1) * 2 * WINDOW] for s in scores],
                    axis=0) + bias_all
                p = _sink_softmax_rows(sm, sinks_ref[layer, SWA_GROUP * kvh + 2 * r + half]).astype(BF16)
                for j in range(nblk):
                    probs[j][half][r] = blk(p, j)
        for j in range(nblk):
            vals = jnp.concatenate(prev_cur(vprep, vc, a, j) + prev_cur(vprep, vc, b, j), axis=0)
            p = jnp.concatenate([jnp.concatenate(probs[j][half], axis=0) for half in range(2)], axis=1)
            og = _dot(p, vals)
            out_cols[a][j] = og[:WINDOW]
            out_cols[b][j] = og[WINDOW:]
    for a in range(2 * SWA_KV_HEADS):
        kprep[a] = blk(kc[a], nblk - 1)
        vprep[a] = blk(vc[a], nblk - 1)
    o = jnp.concatenate([jnp.concatenate(col, axis=0) for col in out_cols], axis=1)
    return o, blk(k, nblk - 1), blk(v, nblk - 1)


def _hgrn_chunk(hq, hf, v, hg, lb, onorm, tri, lvl, s_scr):
    c = HG_CHUNK
    q, g, kg = _hgrn_gates(hq, hf, lb)
    g_hi, g_mid, g_lo = _split3(g)
    G = _dot(tri, g_hi) + _dot(tri, g_mid) + _dot(tri, g_lo)
    t_col = lax.broadcasted_iota(jnp.int32, (c, 1), 0)
    levels = []
    half = HG_DIAG
    while half < c:
        blk = 2 * half
        gb = jnp.concatenate(
            [jnp.broadcast_to(G[p * blk + half - 1:p * blk + half, :], (blk, G.shape[1]))
             for p in range(c // blk)], axis=0)
        second = (t_col & half) != 0
        e = jnp.exp2(jnp.where(second, G - gb, gb - G))
        levels.append(((q * e).astype(BF16), (kg * e).astype(BF16), lvl == len(levels)))
        half = blk
    g_last = G[c - 1:c, :]
    q_in = (q * jnp.exp2(G)).astype(BF16)
    k_out = (kg * jnp.exp2(jnp.minimum(g_last - G, 0.0))).astype(BF16)
    dec_last = jnp.exp2(g_last)
    v_bf = v.astype(BF16)
    nblk = c // HG_DIAG
    row_in_blk = lax.broadcasted_iota(jnp.int32, (nblk, HG_DIAG, HG_DV), 1)
    o_heads = []
    for h in range(HG_HEADS):
        hs = slice(h * HG_DK, (h + 1) * HG_DK)
        s_h = s_scr[h]
        a_off = jnp.zeros((c, c), F32)
        for qt, kt, same in levels:
            a_off = a_off + jnp.where(same, _dot_nt(qt[:, hs], kt[:, hs]), 0.0)
        o_h = _dot(q_in[:, hs], s_h.astype(BF16)) + _dot(a_off.astype(BF16), v_bf[:, hs])
        q3 = q[:, hs].reshape(nblk, HG_DIAG, HG_DK)
        k3 = kg[:, hs].reshape(nblk, HG_DIAG, HG_DK)
        g3 = G[:, hs].reshape(nblk, HG_DIAG, HG_DK)
        v3 = v[:, hs].reshape(nblk, HG_DIAG, HG_DV)
        od = jnp.zeros((nblk, HG_DIAG, HG_DV), F32)
        for s in range(HG_DIAG):
            e = jnp.exp2(jnp.minimum(g3 - g3[:, s:s + 1, :], 0.0))
            a = jnp.sum(q3 * (k3[:, s:s + 1, :] * e), axis=-1, keepdims=True)
            od = od + a * jnp.where(row_in_blk >= s, v3[:, s:s + 1, :], 0.0)
        o_heads.append(o_h + od.reshape(c, HG_DV))
        s_scr[h] = _col_bcast(dec_last[:, hs]) * s_h + _dot_tn(k_out[:, hs], v_bf[:, hs])
    return _hgrn_out(o_heads, onorm, hg)


def _pool_tile(u, carry, pos0, pw_ref, ps_ref):
    tt = u.shape[0]
    x = jnp.concatenate([carry[...], u], axis=0)
    sums = []
    cur = x
    shift = 1
    for gi in range(len(POOL_WINDOWS)):
        cur = cur + pltpu.roll(cur, shift, 0)
        sums.append(cur[POOL_CARRY:, :POOL_GROUP_DIM])
        cur = cur[:, POOL_GROUP_DIM:]
        shift *= 2
    pos = pos0 + lax.broadcasted_iota(jnp.int32, (tt, 1), 0)
    d_groups = []
    for gi, w in enumerate(POOL_WINDOWS):
        cs = slice(gi * POOL_GROUP_DIM, (gi + 1) * POOL_GROUP_DIM)
        cnt = jnp.minimum(w, pos + 1).astype(F32)
        d_groups.append(sums[gi] / cnt - u[:, cs])
    carry[...] = u[tt - POOL_CARRY:, :]
    return _pool_project(d_groups, pw_ref, ps_ref)


def _layer_mix_kernel(sinks_ref, x_ref, nw_ref, w_ref, cos_ref, sin_ref, qn_ref, kn_ref, bd_ref,
                      bias_ref, lb_ref, on_ref, tri_ref, lvl_ref, pw_ref, ps_ref, wb_ref, wo_ref,
                      x1_ref, kc_ref, vc_ref, s_out_ref, ulast_ref,
                      z_ref, kprep, vprep, s_scr, carry, *, layer, steps_per_seq):
    n = pl.program_id(0) % steps_per_seq

    @pl.when(n == 0)
    def _():
        kprep[...] = jnp.zeros_like(kprep)
        vprep[...] = jnp.zeros_like(vprep)
        s_scr[...] = jnp.zeros_like(s_scr)
        carry[...] = jnp.zeros_like(carry)

    qn = qn_ref[...]
    kn = kn_ref[...]
    bd = bd_ref[...]
    lb = lb_ref[...]
    onorm = on_ref[...]
    tri = tri_ref[...]
    lvl = lvl_ref[...]
    band = bias_ref[1]
    first = bias_ref[jnp.minimum(n, 1)]

    _project_tile(x_ref[...], nw_ref[...], w_ref, z_ref)
    nblk = MIX_ROWS // WINDOW
    o_b, k, v = _swa_blocks(z_ref[:, 0:P3], cos_ref[...], sin_ref[...], qn, kn, bd,
                            [first] + [band] * (nblk - 1), sinks_ref, layer, kprep, vprep)
    kc_ref[...] = k
    vc_ref[...] = v
    o_a = []
    for j in range(nblk):
        rz = slice(j * WINDOW, (j + 1) * WINDOW)
        o_a.append(_hgrn_chunk(z_ref[rz, P3:P4], z_ref[rz, P4:P5], z_ref[rz, P5:P6], z_ref[rz, P6:P7],
                               lb, onorm, tri, lvl, s_scr))
    for h in range(HG_HEADS):
        s_out_ref[h] = s_scr[h]
    u = z_ref[:, P7:P8]
    ulast_ref[...] = u[MIX_ROWS - POOL_CARRY:, :]
    o_c = _pool_tile(u, carry, n * MIX_ROWS, pw_ref, ps_ref)
    m = None
    for nb, br in enumerate((jnp.concatenate(o_a, axis=0), o_b, o_c)):
        gate = _sigmoid(z_ref[:, P8 + nb * D_MODEL:P8 + (nb + 1) * D_MODEL])
        term = gate * _dot(br.astype(BF16), wb_ref[nb])
        m = term if m is None else m + term
    x1_ref[...] = x_ref[...] + _dot(m.astype(BF16), wo_ref[...])


def _layer_mix(x, norm_w, w_in_bf, cos, sin, qn, kn, bd, bias, sinks, lower, onorm, tri, lvl, pool_w_bf, pool_scale,
               w_branch_bf, w_o_bf, layer, batch, seq):
    step_rows = MIX_ROWS
    sps = seq // step_rows
    nsteps = batch * sps
    const2 = lambda s: (0, 0)
    lay3 = lambda s: (layer, 0, 0)
    per_seq = lambda s: (s // sps, 0, 0)
    single = pl.Buffered(1)
    return pl.pallas_call(
        functools.partial(_layer_mix_kernel, layer=layer, steps_per_seq=sps),
        grid=(nsteps,),
        in_specs=[
            pl.BlockSpec(memory_space=pltpu.SMEM),
            pl.BlockSpec((step_rows, D_MODEL), lambda s: (s, 0)),
            pl.BlockSpec((None, 1, D_MODEL), lay3),
            pl.BlockSpec((None, D_MODEL, D_IN), lay3, pipeline_mode=single),
            pl.BlockSpec((step_rows, 128), lambda s: (s % sps, 0)),
            pl.BlockSpec((step_rows, 128), lambda s: (s % sps, 0)),
            pl.BlockSpec((1, SWA_Q), const2),
            pl.BlockSpec((1, SWA_KV), const2),
            pl.BlockSpec((256, 256), const2),
            pl.BlockSpec((2, WINDOW, 2 * WINDOW), lambda s: (0, 0, 0)),
            pl.BlockSpec((None, 1, HG_K), lay3),
            pl.BlockSpec((None, 1, HG_DV), lay3),
            pl.BlockSpec((HG_CHUNK, HG_CHUNK), const2),
            pl.BlockSpec((HG_CHUNK, HG_CHUNK), const2),
            pl.BlockSpec((None, len(POOL_WINDOWS), POOL_GROUP_DIM, POOL_GROUP_DIM), lambda s: (layer, 0, 0, 0)),
            pl.BlockSpec((None, 1, POOL_WIDTH), lay3),
            pl.BlockSpec((None, N_BRANCH, BRANCH_WIDTH, D_MODEL), lambda s: (layer, 0, 0, 0), pipeline_mode=single),
            pl.BlockSpec((None, D_MODEL, D_MODEL), lay3, pipeline_mode=single),
        ],
        out_specs=[
            pl.BlockSpec((step_rows, D_MODEL), lambda s: (s, 0)),
            pl.BlockSpec((None, WINDOW, SWA_KV), per_seq),
            pl.BlockSpec((None, WINDOW, SWA_KV), per_seq),
            pl.BlockSpec((None, HG_HEADS, HG_DK, HG_DV), lambda s: (s // sps, 0, 0, 0)),
            pl.BlockSpec((None, POOL_CARRY, POOL_WIDTH), per_seq),
        ],
        out_shape=[
            jax.ShapeDtypeStruct((batch * seq, D_MODEL), F32),
            jax.ShapeDtypeStruct((batch, WINDOW, SWA_KV), F32),
            jax.ShapeDtypeStruct((batch, WINDOW, SWA_KV), F32),
            jax.ShapeDtypeStruct((batch, HG_HEADS, HG_DK, HG_DV), F32),
            jax.ShapeDtypeStruct((batch, POOL_CARRY, POOL_WIDTH), F32),
        ],
        scratch_shapes=[
            pltpu.VMEM((MIX_ROWS, D_IN), F32),
            pltpu.VMEM((2 * SWA_KV_HEADS, WINDOW, 128), BF16),
            pltpu.VMEM((2 * SWA_KV_HEADS, WINDOW, 128), BF16),
            pltpu.VMEM((HG_HEADS, HG_DK, HG_DV), F32),
            pltpu.VMEM((POOL_CARRY, POOL_WIDTH), F32),
        ],
        compiler_params=_cparams("arbitrary"),
        name="layer_mix",
    )(sinks, x, norm_w, w_in_bf, cos, sin, qn, kn, bd, bias, lower, onorm, tri, lvl, pool_w_bf, pool_scale,
      w_branch_bf, w_o_bf)


def _ffn_kernel(x_ref, nw_ref, wu_ref, wd_ref, o_ref):
    x = x_ref[...]
    xn = _rms_rows(x, nw_ref[...]).astype(BF16)
    acc = x
    step = D_MODEL
    for c in range(0, D_FF, step):
        h = _dot(xn, wu_ref[:, c:c + step])
        h = jnp.square(jnp.maximum(h, 0.0)).astype(BF16)
        acc = acc + _dot(h, wd_ref[c:c + step, :])
    o_ref[...] = acc


def _ffn(x, norm_w, w_up_bf, w_down_bf, layer, tm):
    m = x.shape[0]
    row = lambda i: (i, 0)
    return pl.pallas_call(
        _ffn_kernel,
        grid=(m // tm,),
        in_specs=[
            pl.BlockSpec((tm, D_MODEL), row),
            pl.BlockSpec((None, 1, D_MODEL), lambda i: (layer, 0, 0)),
            pl.BlockSpec((None, D_MODEL, D_FF), lambda i: (layer, 0, 0), pipeline_mode=pl.Buffered(1)),
            pl.BlockSpec((None, D_FF, D_MODEL), lambda i: (layer, 0, 0), pipeline_mode=pl.Buffered(1)),
        ],
        out_specs=pl.BlockSpec((tm, D_MODEL), row),
        out_shape=jax.ShapeDtypeStruct((m, D_MODEL), F32),
        compiler_params=_cparams("arbitrary"),
        name="ffn",
    )(x, norm_w, w_up_bf, w_down_bf)


def _rope_tables(pos):
    hd = SWA_HEAD_DIM
    inv = jnp.power(ROPE_THETA, -jnp.arange(0, hd, 2, dtype=F32) / hd)
    ang = pos[:, None] * inv[None, :]
    cos = jnp.cos(ang)
    sin = jnp.sin(ang)
    cos_t = jnp.concatenate([cos, cos, cos, cos], axis=1)
    sin_t = jnp.concatenate([-sin, sin, -sin, sin], axis=1)
    return cos_t, sin_t


def _hgrn_levels():
    t = jnp.arange(HG_CHUNK)[:, None]
    s = jnp.arange(HG_CHUNK)[None, :]
    lvl = jnp.full((HG_CHUNK, HG_CHUNK), -1, jnp.int32)
    half, level = HG_DIAG, 0
    while half < HG_CHUNK:
        shift = (2 * half).bit_length() - 1
        own = ((t >> shift) == (s >> shift)) & ((t & half) != 0) & ((s & half) == 0)
        lvl = jnp.where(own, level, lvl)
        half, level = 2 * half, level + 1
    return lvl


def _prompt_bias():
    i = jnp.arange(WINDOW)[:, None]
    j = jnp.arange(2 * WINDOW)[None, :]
    band = (j > i) & (j <= i + WINDOW)
    first = band & (j >= WINDOW)
    return jnp.where(jnp.stack([first, band]), 0.0, NEG).astype(F32)


def _sample_bias(wb, t_new):
    grp = 8 // t_new
    n_cache = grp * wb
    nkeys = -(-(n_cache + 8) // 128) * 128
    r = jnp.arange(8)[:, None]
    seq_r, t = r // t_new, r % t_new
    j = jnp.arange(nkeys)[None, :]
    cache_ok = (j < n_cache) & (j // wb == seq_r) & (wb + t - j % wb < WINDOW)
    jn = j - n_cache
    new_ok = (jn >= 0) & (jn < 8) & (jn // t_new == seq_r) & (jn % t_new <= t)
    return jnp.where(cache_ok | new_ok, 0.0, NEG).astype(F32)


def _row_tile(m, cap):
    t = min(m, cap)
    while m % t:
        t //= 2
    return t


def kernel(x_prompt, x_sample, state_hgrn, cache_swa_k, cache_swa_v, state_pool, norm_mix, w_in, q_norm, k_norm,
           attn_sinks, hgrn_lb, hgrn_onorm, pool_w, pool_scale, w_branch, w_o, norm_ffn, w_up, w_down):
    depth = w_in.shape[0]
    bp, seq, _ = x_prompt.shape
    bs, t_new, _ = x_sample.shape
    wb = cache_swa_k.shape[2]

    w_in_bf = w_in.astype(BF16)
    w_branch_bf = w_branch.astype(BF16)
    w_o_bf = w_o.astype(BF16)
    w_up_bf = w_up.astype(BF16)
    w_down_bf = w_down.astype(BF16)
    pool_w_bf = pool_w.astype(BF16)
    norm_mix3 = norm_mix.reshape(depth, 1, D_MODEL)
    norm_ffn3 = norm_ffn.reshape(depth, 1, D_MODEL)
    onorm3 = hgrn_onorm.reshape(depth, 1, HG_DV)
    pool_scale3 = pool_scale.reshape(depth, 1, POOL_WIDTH)
    lower3 = _lower_bounds(hgrn_lb).reshape(depth, 1, HG_K)
    cache_k4 = cache_swa_k.reshape(depth, bs, wb, SWA_KV)
    cache_v4 = cache_swa_v.reshape(depth, bs, wb, SWA_KV)

    cos_p, sin_p = _rope_tables(jnp.arange(seq, dtype=F32))
    cos_s, sin_s = _rope_tables(jnp.arange(t_new, dtype=F32) + PAST_LEN)
    bb_swa = SWA_SAMPLE_BB if bs % SWA_SAMPLE_BB == 0 else bs
    cos_s = jnp.tile(cos_s, (bb_swa, 1))
    sin_s = jnp.tile(sin_s, (bb_swa, 1))
    bias_p = _prompt_bias()
    bias_s = _sample_bias(wb, t_new)
    lane = jnp.arange(256)
    bd = (lane[:, None] // SWA_HEAD_DIM == lane[None, :] // SWA_HEAD_DIM).astype(BF16)
    tok = jnp.arange(HG_CHUNK)
    tri = (tok[:, None] >= tok[None, :]).astype(BF16)
    lvl = _hgrn_levels()

    hp = x_prompt.reshape(bp * seq, D_MODEL)
    hs = x_sample.reshape(bs * t_new, D_MODEL)
    tm_p = _row_tile(bp * seq, 512)
    tm_s = _row_tile(bs * t_new, 512)

    sp = ([], [], [], [])
    ss = ([], [], [], [])
    kv_stacks = (cache_k4, cache_v4)
    for l in range(depth):
        qn = jnp.tile(q_norm[l], SWA_HEADS).reshape(1, SWA_Q)
        kn = jnp.tile(k_norm[l], SWA_KV_HEADS).reshape(1, SWA_KV)

        x1, k_last, v_last, s_new, u_last = _layer_mix(
            hp, norm_mix3, w_in_bf, cos_p, sin_p, qn, kn, bd, bias_p, attn_sinks, lower3, onorm3, tri, lvl,
            pool_w_bf, pool_scale3, w_branch_bf, w_o_bf, l, bp, seq)
        hp = _ffn(x1, norm_ffn3, w_up_bf, w_down_bf, l, tm_p)
        sp[0].append(s_new)
        sp[1].append(k_last.reshape(bp, WINDOW, SWA_KV_HEADS, SWA_HEAD_DIM))
        sp[2].append(v_last.reshape(bp, WINDOW, SWA_KV_HEADS, SWA_HEAD_DIM))
        sp[3].append(u_last[:, POOL_CARRY - POOL_BUF:])

        qkv, hq, hf, hi, hg, u, zg = _proj_in(hs, norm_mix3, w_in_bf, l, tm_s)
        o_b, k_stack, v_stack = _swa_sample(qkv, cos_s, sin_s, qn, kn, bd, bias_s, attn_sinks, cache_k4, cache_v4,
                                            kv_stacks, l, bs, t_new)
        kv_stacks = (k_stack, v_stack)
        o_a, s_new = _hgrn_sample(hq, hf, hi, hg, lower3, onorm3, state_hgrn, l, bs, t_new)
        ss[0].append(s_new)
        u3 = u.reshape(bs, t_new, POOL_WIDTH)
        o_c = _pool_sample(jnp.swapaxes(state_pool[l], 0, 1), jnp.swapaxes(u3, 0, 1),
                           pool_w_bf, pool_scale3, l, t_new)
        o_c = jnp.swapaxes(o_c, 0, 1).reshape(bs * t_new, POOL_WIDTH)
        x1 = _merge(hs, o_a, o_b, o_c, zg, w_branch_bf, w_o_bf, l, tm_s)
        hs = _ffn(x1, norm_ffn3, w_up_bf, w_down_bf, l, tm_s)
        ss[3].append(jnp.concatenate([state_pool[l], u3], axis=1)[:, -POOL_BUF:])

    cache_shape = (depth, bs, wb, SWA_KV_HEADS, SWA_HEAD_DIM)
    return (hp.reshape(bp, seq, D_MODEL), hs.reshape(bs, t_new, D_MODEL),
            jnp.stack(sp[0]), jnp.stack(sp[1]), jnp.stack(sp[2]), jnp.stack(sp[3]),
            jnp.stack(ss[0]), kv_stacks[0].reshape(cache_shape), kv_stacks[1].reshape(cache_shape),
            jnp.stack(ss[3]))
```

```python
import functools

import jax
import jax.numpy as jnp
from jax import lax
from jax.experimental import pallas as pl
from jax.experimental.pallas import tpu as pltpu

F32 = jnp.float32
BF16 = jnp.bfloat16

D_MODEL = 1024
PAST_LEN = 8192
SWA_HEADS = 8
SWA_KV_HEADS = 2
SWA_HEAD_DIM = 64
SWA_GROUP = SWA_HEADS // SWA_KV_HEADS
WINDOW = 128
ROPE_THETA = 10000.0
HG_HEADS = 4
HG_DK = 128
HG_DV = 128
LB_FLOOR = 1e-30
POOL_WINDOWS = (2, 4, 8, 16)
POOL_GROUP_DIM = 128
POOL_WIDTH = 512
POOL_BUF = 15
BRANCH_WIDTH = 512
N_BRANCH = 3
D_FF = 4 * D_MODEL
EPS = 1e-6
NEG = -1e30
LOG2E = 1.4426950408889634

SWA_Q = SWA_HEADS * SWA_HEAD_DIM
SWA_KV = SWA_KV_HEADS * SWA_HEAD_DIM
HG_K = HG_HEADS * HG_DK
QKV_W = SWA_Q + 2 * SWA_KV
P3 = QKV_W
P4 = P3 + HG_K
P5 = P4 + HG_K
P6 = P5 + HG_K
P7 = P6 + HG_K
P8 = P7 + POOL_WIDTH
D_IN = P8 + N_BRANCH * D_MODEL

VMEM_LIMIT_BYTES = 56 * 1024 * 1024
HG_CHUNK = 128
HG_DIAG = 8
POOL_CARRY = 16
SAMPLE_BB = 8
MIX_ROWS = 512
SWA_SAMPLE_BB = 16


def _cparams(*sem):
    return pltpu.CompilerParams(dimension_semantics=sem, vmem_limit_bytes=VMEM_LIMIT_BYTES)


def _rms_rows(x, w):
    ms = jnp.mean(x * x, axis=-1, keepdims=True)
    return x * lax.rsqrt(ms + EPS) * w


def _sigmoid(z):
    return 1.0 / (1.0 + jnp.exp2(z * (-LOG2E)))


def _silu(z):
    return z * _sigmoid(z)


def _split3(x):
    hi = x.astype(BF16)
    r1 = x - hi.astype(F32)
    mid = r1.astype(BF16)
    lo = (r1 - mid.astype(F32)).astype(BF16)
    return hi, mid, lo


def _dot(a, b):
    return jnp.dot(a, b, preferred_element_type=F32)


def _dot_nt(a, b):
    return lax.dot_general(a, b, (((1,), (1,)), ((), ())), preferred_element_type=F32)


def _dot_tn(a, b):
    return lax.dot_general(a, b, (((0,), (0,)), ((), ())), preferred_element_type=F32)


def _lower_kernel(lb_ref, o_ref):
    x = lb_ref[...]
    depth = x.shape[0]
    rows = [x[i:i + 1] for i in range(depth)]
    m = rows[0]
    for r in rows[1:]:
        m = jnp.maximum(m, r)
    es = [jnp.exp(r - m) for r in rows]
    tot = es[0]
    for e in es[1:]:
        tot = tot + e
    ps = [e / tot for e in es]
    out = []
    c = None
    for p in ps:
        c = p if c is None else c + p
        out.append(jnp.maximum(c - ps[0], 0.0))
    o_ref[...] = jnp.concatenate(out, axis=0)


def _lower_bounds(hgrn_lb):
    return pl.pallas_call(
        _lower_kernel,
        out_shape=jax.ShapeDtypeStruct(hgrn_lb.shape, F32),
        name="lower_bounds",
    )(hgrn_lb)


def _proj_in_kernel(x_ref, nw_ref, w_ref, qkv_ref, hq_ref, hf_ref, hi_ref, hg_ref, u_ref, zg_ref):
    xn = _rms_rows(x_ref[...], nw_ref[...]).astype(BF16)

    def mm(a, b):
        return _dot(xn, w_ref[:, a:b])

    qkv_ref[...] = mm(0, P3)
    hq_ref[...] = mm(P3, P4)
    hf_ref[...] = mm(P4, P5)
    hi_ref[...] = mm(P5, P6)
    hg_ref[...] = mm(P6, P7)
    u_ref[...] = mm(P7, P8)
    for n in range(N_BRANCH):
        zg_ref[:, n * D_MODEL:(n + 1) * D_MODEL] = mm(P8 + n * D_MODEL, P8 + (n + 1) * D_MODEL)


def _proj_in(x, norm_w, w_in_bf, layer, tm):
    m = x.shape[0]
    widths = (QKV_W, HG_K, HG_K, HG_K, HG_K, POOL_WIDTH, N_BRANCH * D_MODEL)
    return pl.pallas_call(
        _proj_in_kernel,
        grid=(m // tm,),
        in_specs=[
            pl.BlockSpec((tm, D_MODEL), lambda i: (i, 0)),
            pl.BlockSpec((None, 1, D_MODEL), lambda i: (layer, 0, 0)),
            pl.BlockSpec((None, D_MODEL, D_IN), lambda i: (layer, 0, 0), pipeline_mode=pl.Buffered(1)),
        ],
        out_specs=[pl.BlockSpec((tm, w), lambda i: (i, 0)) for w in widths],
        out_shape=[jax.ShapeDtypeStruct((m, w), F32) for w in widths],
        compiler_params=_cparams("arbitrary"),
        name="proj_in",
    )(x, norm_w, w_in_bf)


def _head_norm_rope(x, w_row, cos, sin, bd):
    width = x.shape[1]
    sq = x * x
    hi = sq.astype(BF16)
    lo = (sq - hi.astype(F32)).astype(BF16)
    parts = []
    for c in range(0, width, 256):
        wc = min(256, width - c)
        b = bd[:wc, :wc]
        parts.append(_dot(hi[:, c:c + wc], b) + _dot(lo[:, c:c + wc], b))
    ms = (parts[0] if len(parts) == 1 else jnp.concatenate(parts, axis=1)) * (1.0 / SWA_HEAD_DIM)
    y = x * lax.rsqrt(ms + EPS) * w_row
    reps = width // 128
    cos_t = cos if reps == 1 else jnp.concatenate([cos] * reps, axis=1)
    sin_t = sin if reps == 1 else jnp.concatenate([sin] * reps, axis=1)
    lane = lax.broadcasted_iota(jnp.int32, y.shape, 1)
    first_half = (lane & (SWA_HEAD_DIM - 1)) < (SWA_HEAD_DIM // 2)
    half = SWA_HEAD_DIM // 2
    rot = jnp.where(first_half, pltpu.roll(y, width - half, 1), pltpu.roll(y, half, 1))
    return y * cos_t + rot * sin_t


def _kv_pad(x):
    lo = lax.broadcasted_iota(jnp.int32, x.shape, 1) < SWA_HEAD_DIM
    xr = pltpu.roll(x, SWA_HEAD_DIM, 1)
    return (jnp.where(lo, x, 0.0), jnp.where(lo, 0.0, xr), jnp.where(lo, xr, 0.0), jnp.where(lo, 0.0, x))


def _sink_softmax_rows(sm, sink):
    mx = jnp.maximum(jnp.max(sm, axis=-1, keepdims=True), sink)
    e = jnp.exp(sm - mx)
    den = jnp.sum(e, axis=-1, keepdims=True) + jnp.exp(sink - mx)
    return e * (1.0 / den)


def _swa_sample_kernel(sinks_ref, qkv_ref, cos_ref, sin_ref, qn_ref, kn_ref, bd_ref, bias_ref, kb_ref, vb_ref,
                       kstack_ref, vstack_ref, o_ref, kc_out_ref, vc_out_ref, *, layer, t_new):
    del kstack_ref, vstack_ref
    z = qkv_ref[...]
    cos = cos_ref[...]
    sin = sin_ref[...]
    bd = bd_ref[...]
    q = _head_norm_rope(z[:, :SWA_Q], qn_ref[...], cos, sin, bd) * (SWA_HEAD_DIM ** -0.5)
    k = _head_norm_rope(z[:, SWA_Q:SWA_Q + SWA_KV], kn_ref[...], cos, sin, bd)
    v = z[:, SWA_Q + SWA_KV:]

    rows = z.shape[0]
    ng = rows // 8
    nkeys = bias_ref.shape[1]
    nseq, wb = kb_ref.shape[0], kb_ref.shape[1]
    for cache_ref, out_ref, new in ((kb_ref, kc_out_ref, k), (vb_ref, vc_out_ref, v)):
        flat = cache_ref[...].reshape(nseq * wb, SWA_KV)
        out_ref[...] = pltpu.roll(flat, nseq * wb - t_new, 0).reshape(nseq, wb, SWA_KV)
        for b in range(nseq):
            out_ref[b, wb - t_new:wb, :] = new[b * t_new:(b + 1) * t_new]
    lo = lax.broadcasted_iota(jnp.int32, (rows, 128), 1) < SWA_HEAD_DIM
    qw = []
    for h in range(SWA_HEADS):
        qc = q[:, (h // 2) * 128:(h // 2 + 1) * 128]
        want_hi = h // SWA_GROUP == 1
        if (h % 2 == 1) != want_hi:
            qc = pltpu.roll(qc, SWA_HEAD_DIM, 1)
        qw.append((jnp.where(lo, 0.0, qc) if want_hi else jnp.where(lo, qc, 0.0)).reshape(ng, 8, 128))
    qw = jnp.concatenate(qw, axis=1).astype(BF16)

    def keys_of(cache_ref, new):
        cache = cache_ref[...].reshape(ng, (8 // t_new) * wb, SWA_KV)
        pad = jnp.zeros((ng, nkeys - cache.shape[1] - 8, SWA_KV), F32)
        return jnp.concatenate([cache, new.reshape(ng, 8, SWA_KV), pad], axis=1).astype(BF16)

    s = jnp.einsum('gqd,gkd->gqk', qw, keys_of(kb_ref, k), preferred_element_type=F32)
    bias = bias_ref[...]
    p = jnp.concatenate(
        [_sink_softmax_rows(s[:, h * 8:(h + 1) * 8, :] + bias, sinks_ref[layer, h]).astype(BF16)
         for h in range(SWA_HEADS)], axis=1)
    res = jnp.einsum('gqk,gkd->gqd', p, keys_of(vb_ref, v), preferred_element_type=F32)
    res = res.reshape(ng * SWA_HEADS * 8, 128)
    res_r = pltpu.roll(res, SWA_HEAD_DIM, 1)
    res = res.reshape(ng, SWA_HEADS * 8, 128)
    res_r = res_r.reshape(ng, SWA_HEADS * 8, 128)
    lo3 = lo.reshape(ng, 8, 128)
    cols = []
    for c in range(SWA_HEADS // 2):
        kv_hi = (2 * c) // SWA_GROUP == 1
        even = (res_r if kv_hi else res)[:, (2 * c) * 8:(2 * c + 1) * 8, :]
        odd = (res if kv_hi else res_r)[:, (2 * c + 1) * 8:(2 * c + 2) * 8, :]
        cols.append(jnp.where(lo3, even, odd).reshape(rows, 128))
    o_ref[...] = jnp.concatenate(cols, axis=1)


def _swa_sample(qkv, cos, sin, qn, kn, bd, bias, sinks, kbuf, vbuf, stacked_prev, layer, batch, t_new):
    bb = SWA_SAMPLE_BB if batch % SWA_SAMPLE_BB == 0 else batch
    rb = bb * t_new
    wb = kbuf.shape[2]
    row = lambda i: (i, 0)
    const = lambda i: (0, 0)
    cache_blk = pl.BlockSpec((None, bb, wb, SWA_KV), lambda i: (layer, i, 0, 0))
    any_spec = pl.BlockSpec(memory_space=pl.ANY)
    return pl.pallas_call(
        functools.partial(_swa_sample_kernel, layer=layer, t_new=t_new),
        grid=(batch // bb,),
        in_specs=[
            pl.BlockSpec(memory_space=pltpu.SMEM),
            pl.BlockSpec((rb, QKV_W), row),
            pl.BlockSpec((rb, 128), const),
            pl.BlockSpec((rb, 128), const),
            pl.BlockSpec((1, SWA_Q), const),
            pl.BlockSpec((1, SWA_KV), const),
            pl.BlockSpec((256, 256), const),
            pl.BlockSpec(bias.shape, const),
            cache_blk,
            cache_blk,
            any_spec,
            any_spec,
        ],
        out_specs=[pl.BlockSpec((rb, SWA_Q), row), cache_blk, cache_blk],
        out_shape=[
            jax.ShapeDtypeStruct((batch * t_new, SWA_Q), F32),
            jax.ShapeDtypeStruct(kbuf.shape, F32),
            jax.ShapeDtypeStruct(vbuf.shape, F32),
        ],
        input_output_aliases={10: 1, 11: 2},
        compiler_params=_cparams("arbitrary"),
        name="swa_sample",
    )(sinks, qkv, cos, sin, qn, kn, bd, bias, kbuf, vbuf, stacked_prev[0], stacked_prev[1])


def _hgrn_gates(hq, hf, lb):
    sig_p = _sigmoid(hf)
    om = 1.0 - lb
    log2f = jnp.log(jnp.maximum(lb, LB_FLOOR) + om * sig_p) * LOG2E
    return _silu(hq), log2f, om * (1.0 - sig_p)


def _hgrn_out(o, onorm, hg):
    outs = []
    for h in range(HG_HEADS):
        hs = slice(h * HG_DV, (h + 1) * HG_DV)
        outs.append(_rms_rows(o[h], onorm) * _silu(hg[:, hs]))
    return jnp.concatenate(outs, axis=1)


def _col_bcast(row):
    return jnp.transpose(jnp.broadcast_to(row, (row.shape[1], row.shape[1])))


def _hgrn_sample_kernel(hq_ref, hf_ref, hi_ref, hg_ref, lb_ref, on_ref, s0_ref, stack_ref, o_ref, s_out_ref, *,
                        t_new):
    del stack_ref
    q, g, kg = _hgrn_gates(hq_ref[...], hf_ref[...], lb_ref[...])
    v = hi_ref[...]
    rows = q.shape[0]
    t_idx = lax.broadcasted_iota(jnp.int32, (rows, 1), 0) % t_new

    def down(x, d):
        return pltpu.roll(x, d, 0)

    def up(x, d):
        return pltpu.roll(x, rows - d, 0)

    G = g
    for d in range(1, t_new):
        G = G + jnp.where(t_idx >= d, down(g, d), 0.0)
    rest = jnp.zeros_like(g)
    for d in range(1, t_new):
        rest = rest + jnp.where(t_idx + d < t_new, up(g, d), 0.0)
    q_in = q * jnp.exp2(G)
    k_out = kg * jnp.exp2(rest)
    dec_all = jnp.exp2(G + rest)

    o_intra = [jnp.zeros((rows, HG_DV), F32) for _ in range(HG_HEADS)]
    for d in range(t_new):
        kd = kg if d == 0 else down(kg, d)
        vd = v if d == 0 else down(v, d)
        gd = G if d == 0 else down(G, d)
        w = q * kd * jnp.exp2(jnp.minimum(G - gd, 0.0))
        for h in range(HG_HEADS):
            hs = slice(h * HG_DK, (h + 1) * HG_DK)
            a = jnp.sum(w[:, hs], axis=-1, keepdims=True)
            o_intra[h] = o_intra[h] + jnp.where(t_idx >= d, a, 0.0) * vd[:, hs]

    dec_hi = dec_all.astype(BF16).astype(F32)
    dec_r = dec_all - dec_hi
    dec_mid = dec_r.astype(BF16).astype(F32)
    dec_lo = (dec_r - dec_mid).astype(BF16).astype(F32)
    dec_parts = jnp.where(t_idx == 0, dec_hi, jnp.where(t_idx == 1, dec_mid, jnp.where(t_idx == 2, dec_lo, 0.0)))
    seq_of_row = lax.broadcasted_iota(jnp.int32, (rows, 1), 0) // t_new
    ones = jnp.ones((rows, HG_DV), BF16)
    v_bf = v.astype(BF16)
    o = []
    for h in range(HG_HEADS):
        hs = slice(h * HG_DK, (h + 1) * HG_DK)
        o_h = o_intra[h]
        for b in range(s0_ref.shape[0]):
            mine = seq_of_row == b
            s_h = s0_ref[b, h]
            o_h = o_h + _dot(jnp.where(mine, q_in[:, hs], 0.0).astype(BF16), s_h.astype(BF16))
            dec_mat = _dot_tn(jnp.where(mine, dec_parts[:, hs], 0.0).astype(BF16), ones)
            upd = _dot_tn(jnp.where(mine, k_out[:, hs], 0.0).astype(BF16), v_bf[:, hs])
            s_out_ref[b, h] = dec_mat * s_h + upd
        o.append(o_h)
    o_ref[...] = _hgrn_out(o, on_ref[...], hg_ref[...])


def _hgrn_sample(hq, hf, hi, hg, lower, onorm, state, stacked_prev, layer, batch, t_new):
    assert t_new >= 3
    rb = SAMPLE_BB * t_new
    blk = pl.BlockSpec((rb, HG_K), lambda i: (i, 0))
    st_blk = pl.BlockSpec((None, SAMPLE_BB, HG_HEADS, HG_DK, HG_DV), lambda i: (layer, i, 0, 0, 0))
    return pl.pallas_call(
        functools.partial(_hgrn_sample_kernel, t_new=t_new),
        grid=(batch // SAMPLE_BB,),
        in_specs=[
            blk, blk, blk, blk,
            pl.BlockSpec((None, 1, HG_K), lambda i: (layer, 0, 0)),
            pl.BlockSpec((None, 1, HG_DV), lambda i: (layer, 0, 0)),
            st_blk,
            pl.BlockSpec(memory_space=pl.ANY),
        ],
        out_specs=[blk, st_blk],
        out_shape=[
            jax.ShapeDtypeStruct((batch * t_new, HG_K), F32),
            jax.ShapeDtypeStruct(state.shape, state.dtype),
        ],
        input_output_aliases={7: 1},
        compiler_params=_cparams("arbitrary"),
        name="hgrn_sample",
    )(hq, hf, hi, hg, lower, onorm, state, stacked_prev)


def _pool_project(d_groups, pw_ref, ps_ref):
    outs = []
    for gi, d in enumerate(d_groups):
        cs = slice(gi * POOL_GROUP_DIM, (gi + 1) * POOL_GROUP_DIM)
        outs.append(_dot(d.astype(BF16), pw_ref[gi]) * ps_ref[:, cs])
    return jnp.concatenate(outs, axis=1)


def _pool_sample_kernel(pb_ref, u_ref, pw_ref, ps_ref, o_ref, *, t_new):
    def row(i):
        return pb_ref[i] if i < POOL_BUF else u_ref[i - POOL_BUF]

    outs = []
    for t in range(t_new):
        e = POOL_BUF + t
        d_groups = []
        for gi, w in enumerate(POOL_WINDOWS):
            cs = slice(gi * POOL_GROUP_DIM, (gi + 1) * POOL_GROUP_DIM)
            lo = max(e + 1 - w, 0)
            acc = row(lo)[:, cs]
            for i in range(lo + 1, e + 1):
                acc = acc + row(i)[:, cs]
            d_groups.append(acc / float(e + 1 - lo) - u_ref[t][:, cs])
        outs.append(_pool_project(d_groups, pw_ref, ps_ref))
    for t in range(t_new):
        o_ref[t] = outs[t]


def _pool_sample(pbuf_t, u_t, pool_w_bf, pool_scale, layer, t_new):
    batch = u_t.shape[1]
    return pl.pallas_call(
        functools.partial(_pool_sample_kernel, t_new=t_new),
        grid=(1,),
        in_specs=[
            pl.BlockSpec((POOL_BUF, batch, POOL_WIDTH), lambda i: (0, 0, 0)),
            pl.BlockSpec((t_new, batch, POOL_WIDTH), lambda i: (0, 0, 0)),
            pl.BlockSpec((None, len(POOL_WINDOWS), POOL_GROUP_DIM, POOL_GROUP_DIM), lambda i: (layer, 0, 0, 0)),
            pl.BlockSpec((None, 1, POOL_WIDTH), lambda i: (layer, 0, 0)),
        ],
        out_specs=pl.BlockSpec((t_new, batch, POOL_WIDTH), lambda i: (0, 0, 0)),
        out_shape=jax.ShapeDtypeStruct((t_new, batch, POOL_WIDTH), F32),
        compiler_params=_cparams("arbitrary"),
        name="pool_sample",
    )(pbuf_t, u_t, pool_w_bf, pool_scale)


def _merge_kernel(x_ref, oa_ref, ob_ref, oc_ref, zg_ref, wb_ref, wo_ref, o_ref):
    m = None
    for n, br in enumerate((oa_ref, ob_ref, oc_ref)):
        proj = _dot(br[...].astype(BF16), wb_ref[n])
        gate = _sigmoid(zg_ref[:, n * D_MODEL:(n + 1) * D_MODEL])
        m = gate * proj if m is None else m + gate * proj
    o_ref[...] = x_ref[...] + _dot(m.astype(BF16), wo_ref[...])


def _merge(x, oa, ob, oc, zg, w_branch_bf, w_o_bf, layer, tm):
    m = x.shape[0]
    row = lambda i: (i, 0)
    return pl.pallas_call(
        _merge_kernel,
        grid=(m // tm,),
        in_specs=[
            pl.BlockSpec((tm, D_MODEL), row),
            pl.BlockSpec((tm, BRANCH_WIDTH), row),
            pl.BlockSpec((tm, BRANCH_WIDTH), row),
            pl.BlockSpec((tm, BRANCH_WIDTH), row),
            pl.BlockSpec((tm, N_BRANCH * D_MODEL), row),
            pl.BlockSpec((None, N_BRANCH, BRANCH_WIDTH, D_MODEL), lambda i: (layer, 0, 0, 0)),
            pl.BlockSpec((None, D_MODEL, D_MODEL), lambda i: (layer, 0, 0)),
        ],
        out_specs=pl.BlockSpec((tm, D_MODEL), row),
        out_shape=jax.ShapeDtypeStruct((m, D_MODEL), F32),
        compiler_params=_cparams("arbitrary"),
        name="merge",
    )(x, oa, ob, oc, zg, w_branch_bf, w_o_bf)


_Z_GROUPS = ((0, P3), (P3, P7), (P7, P8)) + tuple(
    (P8 + n * D_MODEL, P8 + (n + 1) * D_MODEL) for n in range(N_BRANCH))


def _project_tile(x, nw, w_ref, z_ref):
    xn = _rms_rows(x, nw).astype(BF16)
    for a, b in _Z_GROUPS:
        z_ref[:, a:b] = _dot(xn, w_ref[:, a:b])


def _swa_blocks(z, cos, sin, qn, kn, bd, biases, sinks_ref, layer, kprep, vprep):
    nblk = len(biases)
    q = _head_norm_rope(z[:, :SWA_Q], qn, cos, sin, bd) * (SWA_HEAD_DIM ** -0.5)
    k = _head_norm_rope(z[:, SWA_Q:SWA_Q + SWA_KV], kn, cos, sin, bd)
    v = z[:, SWA_Q + SWA_KV:]
    kc = [a.astype(BF16) for a in _kv_pad(k)]
    vc = [a.astype(BF16) for a in _kv_pad(v)]

    def blk(x, j):
        return x[j * WINDOW:(j + 1) * WINDOW]

    def prev_cur(prep, cur, pad, j):
        return (prep[pad] if j == 0 else blk(cur[pad], j - 1)), blk(cur[pad], j)

    bias_all = jnp.concatenate(biases, axis=0)
    out_cols = [[None] * nblk for _ in range(SWA_HEADS // 2)]
    for kvh in range(SWA_KV_HEADS):
        a, b = 2 * kvh, 2 * kvh + 1
        scores = []
        for j in range(nblk):
            keys = jnp.concatenate(prev_cur(kprep, kc, a, j) + prev_cur(kprep, kc, b, j), axis=0)
            qq = jnp.concatenate([blk(q[:, a * 128:(a + 1) * 128], j), blk(q[:, b * 128:(b + 1) * 128], j)], axis=0)
            scores.append(_dot_nt(qq.astype(BF16), keys))
        probs = [[[None, None] for _ in range(2)] for _ in range(nblk)]
        for half in range(2):
            for r in range(2):
                sm = jnp.concatenate(
                    [s[r * WINDOW:(r + 1) * WINDOW, half * 2 * WINDOW:(half + 1) * 2 * WINDOW] for s in scores],
                    axis=0) + bias_all
                p = _sink_softmax_rows(sm, sinks_ref[layer, SWA_GROUP * kvh + 2 * r + half]).astype(BF16)
                for j in range(nblk):
                    probs[j][half][r] = blk(p, j)
        for j in range(nblk):
            vals = jnp.concatenate(prev_cur(vprep, vc, a, j) + prev_cur(vprep, vc, b, j), axis=0)
            p = jnp.concatenate([jnp.concatenate(probs[j][half], axis=0) for half in range(2)], axis=1)
            og = _dot(p, vals)
            out_cols[a][j] = og[:WINDOW]
            out_cols[b][j] = og[WINDOW:]
    for a in range(2 * SWA_KV_HEADS):
        kprep[a] = blk(kc[a], nblk - 1)
        vprep[a] = blk(vc[a], nblk - 1)
    o = jnp.concatenate([jnp.concatenate(col, axis=0) for col in out_cols], axis=1)
    return o, blk(k, nblk - 1), blk(v, nblk - 1)


def _hgrn_chunk(hq, hf, v, hg, lb, onorm, tri, lvl, s_scr):
    c = HG_CHUNK
    q, g, kg = _hgrn_gates(hq, hf, lb)
    g_hi, g_mid, g_lo = _split3(g)
    G = _dot(tri, g_hi) + _dot(tri, g_mid) + _dot(tri, g_lo)
    t_col = lax.broadcasted_iota(jnp.int32, (c, 1), 0)
    levels = []
    half = HG_DIAG
    while half < c:
        blk = 2 * half
        gb = jnp.concatenate(
            [jnp.broadcast_to(G[p * blk + half - 1:p * blk + half, :], (blk, G.shape[1]))
             for p in range(c // blk)], axis=0)
        second = (t_col & half) != 0
        e = jnp.exp2(jnp.where(second, G - gb, gb - G))
        levels.append(((q * e).astype(BF16), (kg * e).astype(BF16), lvl == len(levels)))
        half = blk
    g_last = G[c - 1:c, :]
    q_in = (q * jnp.exp2(G)).astype(BF16)
    k_out = (kg * jnp.exp2(jnp.minimum(g_last - G, 0.0))).astype(BF16)
    dec_last = jnp.exp2(g_last)
    v_bf = v.astype(BF16)
    nblk = c // HG_DIAG
    row_in_blk = lax.broadcasted_iota(jnp.int32, (nblk, HG_DIAG, HG_DV), 1)
    o_heads = []
    for h in range(HG_HEADS):
        hs = slice(h * HG_DK, (h + 1) * HG_DK)
        s_h = s_scr[h]
        a_off = jnp.zeros((c, c), F32)
        for qt, kt, same in levels:
            a_off = a_off + jnp.where(same, _dot_nt(qt[:, hs], kt[:, hs]), 0.0)
        o_h = _dot(q_in[:, hs], s_h.astype(BF16)) + _dot(a_off.astype(BF16), v_bf[:, hs])
        q3 = q[:, hs].reshape(nblk, HG_DIAG, HG_DK)
        k3 = kg[:, hs].reshape(nblk, HG_DIAG, HG_DK)
        g3 = G[:, hs].reshape(nblk, HG_DIAG, HG_DK)
        v3 = v[:, hs].reshape(nblk, HG_DIAG, HG_DV)
        od = jnp.zeros((nblk, HG_DIAG, HG_DV), F32)
        for s in range(HG_DIAG):
            e = jnp.exp2(jnp.minimum(g3 - g3[:, s:s + 1, :], 0.0))
            a = jnp.sum(q3 * (k3[:, s:s + 1, :] * e), axis=-1, keepdims=True)
            od = od + a * jnp.where(row_in_blk >= s, v3[:, s:s + 1, :], 0.0)
        o_heads.append(o_h + od.reshape(c, HG_DV))
        s_scr[h] = _col_bcast(dec_last[:, hs]) * s_h + _dot_tn(k_out[:, hs], v_bf[:, hs])
    return _hgrn_out(o_heads, onorm, hg)


def _pool_tile(u, carry, pos0, pw_ref, ps_ref):
    tt = u.shape[0]
    x = jnp.concatenate([carry[...], u], axis=0)
    sums = []
    cur = x
    shift = 1
    for gi in range(len(POOL_WINDOWS)):
        cur = cur + pltpu.roll(cur, shift, 0)
        sums.append(cur[POOL_CARRY:, :POOL_GROUP_DIM])
        cur = cur[:, POOL_GROUP_DIM:]
        shift *= 2
    pos = pos0 + lax.broadcasted_iota(jnp.int32, (tt, 1), 0)
    d_groups = []
    for gi, w in enumerate(POOL_WINDOWS):
        cs = slice(gi * POOL_GROUP_DIM, (gi + 1) * POOL_GROUP_DIM)
        cnt = jnp.minimum(w, pos + 1).astype(F32)
        d_groups.append(sums[gi] / cnt - u[:, cs])
    carry[...] = u[tt - POOL_CARRY:, :]
    return _pool_project(d_groups, pw_ref, ps_ref)


def _layer_mix_kernel(sinks_ref, x_ref, nw_ref, w_ref, cos_ref, sin_ref, qn_ref, kn_ref, bd_ref,
                      bias_ref, lb_ref, on_ref, tri_ref, lvl_ref, pw_ref, ps_ref, wb_ref, wo_ref,
                      x1_ref, kc_ref, vc_ref, s_out_ref, ulast_ref,
                      z_ref, kprep, vprep, s_scr, carry, *, layer, steps_per_seq):
    n = pl.program_id(0) % steps_per_seq

    @pl.when(n == 0)
    def _():
        kprep[...] = jnp.zeros_like(kprep)
        vprep[...] = jnp.zeros_like(vprep)
        s_scr[...] = jnp.zeros_like(s_scr)
        carry[...] = jnp.zeros_like(carry)

    qn = qn_ref[...]
    kn = kn_ref[...]
    bd = bd_ref[...]
    lb = lb_ref[...]
    onorm = on_ref[...]
    tri = tri_ref[...]
    lvl = lvl_ref[...]
    band = bias_ref[1]
    first = bias_ref[jnp.minimum(n, 1)]

    _project_tile(x_ref[...], nw_ref[...], w_ref, z_ref)
    nblk = MIX_ROWS // WINDOW
    o_b, k, v = _swa_blocks(z_ref[:, 0:P3], cos_ref[...], sin_ref[...], qn, kn, bd,
                            [first] + [band] * (nblk - 1), sinks_ref, layer, kprep, vprep)
    kc_ref[...] = k
    vc_ref[...] = v
    o_a = []
    for j in range(nblk):
        rz = slice(j * WINDOW, (j + 1) * WINDOW)
        o_a.append(_hgrn_chunk(z_ref[rz, P3:P4], z_ref[rz, P4:P5], z_ref[rz, P5:P6], z_ref[rz, P6:P7],
                               lb, onorm, tri, lvl, s_scr))
    for h in range(HG_HEADS):
        s_out_ref[h] = s_scr[h]
    u = z_ref[:, P7:P8]
    ulast_ref[...] = u[MIX_ROWS - POOL_CARRY:, :]
    o_c = _pool_tile(u, carry, n * MIX_ROWS, pw_ref, ps_ref)
    m = None
    for nb, br in enumerate((jnp.concatenate(o_a, axis=0), o_b, o_c)):
        gate = _sigmoid(z_ref[:, P8 + nb * D_MODEL:P8 + (nb + 1) * D_MODEL])
        term = gate * _dot(br.astype(BF16), wb_ref[nb])
        m = term if m is None else m + term
    x1_ref[...] = x_ref[...] + _dot(m.astype(BF16), wo_ref[...])


def _layer_mix(x, norm_w, w_in_bf, cos, sin, qn, kn, bd, bias, sinks, lower, onorm, tri, lvl, pool_w_bf, pool_scale,
               w_branch_bf, w_o_bf, layer, batch, seq):
    step_rows = MIX_ROWS
    sps = seq // step_rows
    nsteps = batch * sps
    const2 = lambda s: (0, 0)
    lay3 = lambda s: (layer, 0, 0)
    per_seq = lambda s: (s // sps, 0, 0)
    single = pl.Buffered(1)
    return pl.pallas_call(
        functools.partial(_layer_mix_kernel, layer=layer, steps_per_seq=sps),
        grid=(nsteps,),
        in_specs=[
            pl.BlockSpec(memory_space=pltpu.SMEM),
            pl.BlockSpec((step_rows, D_MODEL), lambda s: (s, 0)),
            pl.BlockSpec((None, 1, D_MODEL), lay3),
            pl.BlockSpec((None, D_MODEL, D_IN), lay3, pipeline_mode=single),
            pl.BlockSpec((step_rows, 128), lambda s: (s % sps, 0)),
            pl.BlockSpec((step_rows, 128), lambda s: (s % sps, 0)),
            pl.BlockSpec((1, SWA_Q), const2),
            pl.BlockSpec((1, SWA_KV), const2),
            pl.BlockSpec((256, 256), const2),
            pl.BlockSpec((2, WINDOW, 2 * WINDOW), lambda s: (0, 0, 0)),
            pl.BlockSpec((None, 1, HG_K), lay3),
            pl.BlockSpec((None, 1, HG_DV), lay3),
            pl.BlockSpec((HG_CHUNK, HG_CHUNK), const2),
            pl.BlockSpec((HG_CHUNK, HG_CHUNK), const2),
            pl.BlockSpec((None, len(POOL_WINDOWS), POOL_GROUP_DIM, POOL_GROUP_DIM), lambda s: (layer, 0, 0, 0)),
            pl.BlockSpec((None, 1, POOL_WIDTH), lay3),
            pl.BlockSpec((None, N_BRANCH, BRANCH_WIDTH, D_MODEL), lambda s: (layer, 0, 0, 0), pipeline_mode=single),
            pl.BlockSpec((None, D_MODEL, D_MODEL), lay3, pipeline_mode=single),
        ],
        out_specs=[
            pl.BlockSpec((step_rows, D_MODEL), lambda s: (s, 0)),
            pl.BlockSpec((None, WINDOW, SWA_KV), per_seq),
            pl.BlockSpec((None, WINDOW, SWA_KV), per_seq),
            pl.BlockSpec((None, HG_HEADS, HG_DK, HG_DV), lambda s: (s // sps, 0, 0, 0)),
            pl.BlockSpec((None, POOL_CARRY, POOL_WIDTH), per_seq),
        ],
        out_shape=[
            jax.ShapeDtypeStruct((batch * seq, D_MODEL), F32),
            jax.ShapeDtypeStruct((batch, WINDOW, SWA_KV), F32),
            jax.ShapeDtypeStruct((batch, WINDOW, SWA_KV), F32),
            jax.ShapeDtypeStruct((batch, HG_HEADS, HG_DK, HG_DV), F32),
            jax.ShapeDtypeStruct((batch, POOL_CARRY, POOL_WIDTH), F32),
        ],
        scratch_shapes=[
            pltpu.VMEM((MIX_ROWS, D_IN), F32),
            pltpu.VMEM((2 * SWA_KV_HEADS, WINDOW, 128), BF16),
            pltpu.VMEM((2 * SWA_KV_HEADS, WINDOW, 128), BF16),
            pltpu.VMEM((HG_HEADS, HG_DK, HG_DV), F32),
            pltpu.VMEM((POOL_CARRY, POOL_WIDTH), F32),
        ],
        compiler_params=_cparams("arbitrary"),
        name="layer_mix",
    )(sinks, x, norm_w, w_in_bf, cos, sin, qn, kn, bd, bias, lower, onorm, tri, lvl, pool_w_bf, pool_scale,
      w_branch_bf, w_o_bf)


def _ffn_kernel(x_ref, nw_ref, wu_ref, wd_ref, o_ref):
    x = x_ref[...]
    xn = _rms_rows(x, nw_ref[...]).astype(BF16)
    acc = x
    step = D_MODEL
    for c in range(0, D_FF, step):
        h = _dot(xn, wu_ref[:, c:c + step])
        h = jnp.square(jnp.maximum(h, 0.0)).astype(BF16)
        acc = acc + _dot(h, wd_ref[c:c + step, :])
    o_ref[...] = acc


def _ffn(x, norm_w, w_up_bf, w_down_bf, layer, tm):
    m = x.shape[0]
    row = lambda i: (i, 0)
    return pl.pallas_call(
        _ffn_kernel,
        grid=(m // tm,),
        in_specs=[
            pl.BlockSpec((tm, D_MODEL), row),
            pl.BlockSpec((None, 1, D_MODEL), lambda i: (layer, 0, 0)),
            pl.BlockSpec((None, D_MODEL, D_FF), lambda i: (layer, 0, 0), pipeline_mode=pl.Buffered(1)),
            pl.BlockSpec((None, D_FF, D_MODEL), lambda i: (layer, 0, 0), pipeline_mode=pl.Buffered(1)),
        ],
        out_specs=pl.BlockSpec((tm, D_MODEL), row),
        out_shape=jax.ShapeDtypeStruct((m, D_MODEL), F32),
        compiler_params=_cparams("arbitrary"),
        name="ffn",
    )(x, norm_w, w_up_bf, w_down_bf)


def _rope_tables(pos):
    hd = SWA_HEAD_DIM
    inv = jnp.power(ROPE_THETA, -jnp.arange(0, hd, 2, dtype=F32) / hd)
    ang = pos[:, None] * inv[None, :]
    cos = jnp.cos(ang)
    sin = jnp.sin(ang)
    cos_t = jnp.concatenate([cos, cos, cos, cos], axis=1)
    sin_t = jnp.concatenate([-sin, sin, -sin, sin], axis=1)
    return cos_t, sin_t


def _hgrn_levels():
    t = jnp.arange(HG_CHUNK)[:, None]
    s = jnp.arange(HG_CHUNK)[None, :]
    lvl = jnp.full((HG_CHUNK, HG_CHUNK), -1, jnp.int32)
    half, level = HG_DIAG, 0
    while half < HG_CHUNK:
        shift = (2 * half).bit_length() - 1
        own = ((t >> shift) == (s >> shift)) & ((t & half) != 0) & ((s & half) == 0)
        lvl = jnp.where(own, level, lvl)
        half, level = 2 * half, level + 1
    return lvl


def _prompt_bias():
    i = jnp.arange(WINDOW)[:, None]
    j = jnp.arange(2 * WINDOW)[None, :]
    band = (j > i) & (j <= i + WINDOW)
    first = band & (j >= WINDOW)
    return jnp.where(jnp.stack([first, band]), 0.0, NEG).astype(F32)


def _sample_bias(wb, t_new):
    grp = 8 // t_new
    n_cache = grp * wb
    nkeys = -(-(n_cache + 8) // 128) * 128
    r = jnp.arange(8)[:, None]
    seq_r, t = r // t_new, r % t_new
    j = jnp.arange(nkeys)[None, :]
    cache_ok = (j < n_cache) & (j // wb == seq_r) & (wb + t - j % wb < WINDOW)
    jn = j - n_cache
    new_ok = (jn >= 0) & (jn < 8) & (jn // t_new == seq_r) & (jn % t_new <= t)
    return jnp.where(cache_ok | new_ok, 0.0, NEG).astype(F32)


def _row_tile(m, cap):
    t = min(m, cap)
    while m % t:
        t //= 2
    return t


def kernel(x_prompt, x_sample, state_hgrn, cache_swa_k, cache_swa_v, state_pool, norm_mix, w_in, q_norm, k_norm,
           attn_sinks, hgrn_lb, hgrn_onorm, pool_w, pool_scale, w_branch, w_o, norm_ffn, w_up, w_down):
    depth = w_in.shape[0]
    bp, seq, _ = x_prompt.shape
    bs, t_new, _ = x_sample.shape
    wb = cache_swa_k.shape[2]

    w_in_bf = w_in.astype(BF16)
    w_branch_bf = w_branch.astype(BF16)
    w_o_bf = w_o.astype(BF16)
    w_up_bf = w_up.astype(BF16)
    w_down_bf = w_down.astype(BF16)
    pool_w_bf = pool_w.astype(BF16)
    norm_mix3 = norm_mix.reshape(depth, 1, D_MODEL)
    norm_ffn3 = norm_ffn.reshape(depth, 1, D_MODEL)
    onorm3 = hgrn_onorm.reshape(depth, 1, HG_DV)
    pool_scale3 = pool_scale.reshape(depth, 1, POOL_WIDTH)
    lower3 = _lower_bounds(hgrn_lb).reshape(depth, 1, HG_K)
    cache_k4 = cache_swa_k.reshape(depth, bs, wb, SWA_KV)
    cache_v4 = cache_swa_v.reshape(depth, bs, wb, SWA_KV)

    cos_p, sin_p = _rope_tables(jnp.arange(seq, dtype=F32))
    cos_s, sin_s = _rope_tables(jnp.arange(t_new, dtype=F32) + PAST_LEN)
    bb_swa = SWA_SAMPLE_BB if bs % SWA_SAMPLE_BB == 0 else bs
    cos_s = jnp.tile(cos_s, (bb_swa, 1))
    sin_s = jnp.tile(sin_s, (bb_swa, 1))
    bias_p = _prompt_bias()
    bias_s = _sample_bias(wb, t_new)
    lane = jnp.arange(256)
    bd = (lane[:, None] // SWA_HEAD_DIM == lane[None, :] // SWA_HEAD_DIM).astype(BF16)
    tok = jnp.arange(HG_CHUNK)
    tri = (tok[:, None] >= tok[None, :]).astype(BF16)
    lvl = _hgrn_levels()

    hp = x_prompt.reshape(bp * seq, D_MODEL)
    hs = x_sample.reshape(bs * t_new, D_MODEL)
    tm_p = _row_tile(bp * seq, 512)
    tm_s = _row_tile(bs * t_new, 512)

    sp = ([], [], [], [])
    ss = ([], [], [], [])
    kv_stacks = (cache_k4, cache_v4)
    s_stack = jnp.zeros(state_hgrn.shape, state_hgrn.dtype)
    for l in range(depth):
        qn = jnp.tile(q_norm[l], SWA_HEADS).reshape(1, SWA_Q)
        kn = jnp.tile(k_norm[l], SWA_KV_HEADS).reshape(1, SWA_KV)

        x1, k_last, v_last, s_new, u_last = _layer_mix(
            hp, norm_mix3, w_in_bf, cos_p, sin_p, qn, kn, bd, bias_p, attn_sinks, lower3, onorm3, tri, lvl,
            pool_w_bf, pool_scale3, w_branch_bf, w_o_bf, l, bp, seq)
        hp = _ffn(x1, norm_ffn3, w_up_bf, w_down_bf, l, tm_p)
        sp[0].append(s_new)
        sp[1].append(k_last.reshape(bp, WINDOW, SWA_KV_HEADS, SWA_HEAD_DIM))
        sp[2].append(v_last.reshape(bp, WINDOW, SWA_KV_HEADS, SWA_HEAD_DIM))
        sp[3].append(u_last[:, POOL_CARRY - POOL_BUF:])

        qkv, hq, hf, hi, hg, u, zg = _proj_in(hs, norm_mix3, w_in_bf, l, tm_s)
        o_b, k_stack, v_stack = _swa_sample(qkv, cos_s, sin_s, qn, kn, bd, bias_s, attn_sinks, cache_k4, cache_v4,
                                            kv_stacks, l, bs, t_new)
        kv_stacks = (k_stack, v_stack)
        o_a, s_stack = _hgrn_sample(hq, hf, hi, hg, lower3, onorm3, state_hgrn, s_stack, l, bs, t_new)
        u3 = u.reshape(bs, t_new, POOL_WIDTH)
        o_c = _pool_sample(jnp.swapaxes(state_pool[l], 0, 1), jnp.swapaxes(u3, 0, 1),
                           pool_w_bf, pool_scale3, l, t_new)
        o_c = jnp.swapaxes(o_c, 0, 1).reshape(bs * t_new, POOL_WIDTH)
        x1 = _merge(hs, o_a, o_b, o_c, zg, w_branch_bf, w_o_bf, l, tm_s)
        hs = _ffn(x1, norm_ffn3, w_up_bf, w_down_bf, l, tm_s)
        ss[3].append(jnp.concatenate([state_pool[l], u3], axis=1)[:, -POOL_BUF:])

    cache_shape = (depth, bs, wb, SWA_KV_HEADS, SWA_HEAD_DIM)
    return (hp.reshape(bp, seq, D_MODEL), hs.reshape(bs, t_new, D_MODEL),
            jnp.stack(sp[0]), jnp.stack(sp[1]), jnp.stack(sp[2]), jnp.stack(sp[3]),
            s_stack, kv_stacks[0].reshape(cache_shape), kv_stacks[1].reshape(cache_shape),
            jnp.stack(ss[3]))
```

```python
import functools

import jax
import jax.numpy as jnp
from jax import lax
from jax.experimental import pallas as pl
from jax.experimental.pallas import tpu as pltpu

F32 = jnp.float32
BF16 = jnp.bfloat16

D_MODEL = 1024
PAST_LEN = 8192
SWA_HEADS = 8
SWA_KV_HEADS = 2
SWA_HEAD_DIM = 64
SWA_GROUP = SWA_HEADS // SWA_KV_HEADS
WINDOW = 128
ROPE_THETA = 10000.0
HG_HEADS = 4
HG_DK = 128
HG_DV = 128
LB_FLOOR = 1e-30
POOL_WINDOWS = (2, 4, 8, 16)
POOL_GROUP_DIM = 128
POOL_WIDTH = 512
POOL_BUF = 15
BRANCH_WIDTH = 512
N_BRANCH = 3
D_FF = 4 * D_MODEL
EPS = 1e-6
NEG = -1e30
LOG2E = 1.4426950408889634

SWA_Q = SWA_HEADS * SWA_HEAD_DIM
SWA_KV = SWA_KV_HEADS * SWA_HEAD_DIM
HG_K = HG_HEADS * HG_DK
QKV_W = SWA_Q + 2 * SWA_KV
P3 = QKV_W
P4 = P3 + HG_K
P5 = P4 + HG_K
P6 = P5 + HG_K
P7 = P6 + HG_K
P8 = P7 + POOL_WIDTH
D_IN = P8 + N_BRANCH * D_MODEL

VMEM_LIMIT_BYTES = 56 * 1024 * 1024
HG_CHUNK = 128
HG_DIAG = 8
POOL_CARRY = 16
SAMPLE_BB = 8
MIX_ROWS = 512
SWA_SAMPLE_BB = 16


def _cparams(*sem):
    return pltpu.CompilerParams(dimension_semantics=sem, vmem_limit_bytes=VMEM_LIMIT_BYTES)


def _rms_rows(x, w):
    ms = jnp.mean(x * x, axis=-1, keepdims=True)
    return x * lax.rsqrt(ms + EPS) * w


def _sigmoid(z):
    return 1.0 / (1.0 + jnp.exp2(z * (-LOG2E)))


def _silu(z):
    return z * _sigmoid(z)


def _split3(x):
    hi = x.astype(BF16)
    r1 = x - hi.astype(F32)
    mid = r1.astype(BF16)
    lo = (r1 - mid.astype(F32)).astype(BF16)
    return hi, mid, lo


def _dot(a, b):
    return jnp.dot(a, b, preferred_element_type=F32)


def _dot_nt(a, b):
    return lax.dot_general(a, b, (((1,), (1,)), ((), ())), preferred_element_type=F32)


def _dot_tn(a, b):
    return lax.dot_general(a, b, (((0,), (0,)), ((), ())), preferred_element_type=F32)


def _lower_kernel(lb_ref, o_ref):
    x = lb_ref[...]
    depth = x.shape[0]
    rows = [x[i:i + 1] for i in range(depth)]
    m = rows[0]
    for r in rows[1:]:
        m = jnp.maximum(m, r)
    es = [jnp.exp(r - m) for r in rows]
    tot = es[0]
    for e in es[1:]:
        tot = tot + e
    ps = [e / tot for e in es]
    out = []
    c = None
    for p in ps:
        c = p if c is None else c + p
        out.append(jnp.maximum(c - ps[0], 0.0))
    o_ref[...] = jnp.concatenate(out, axis=0)


def _lower_bounds(hgrn_lb):
    return pl.pallas_call(
        _lower_kernel,
        out_shape=jax.ShapeDtypeStruct(hgrn_lb.shape, F32),
        name="lower_bounds",
    )(hgrn_lb)


def _proj_in_kernel(x_ref, nw_ref, w_ref, qkv_ref, hq_ref, hf_ref, hi_ref, hg_ref, u_ref, zg_ref):
    xn = _rms_rows(x_ref[...], nw_ref[...]).astype(BF16)

    def mm(a, b):
        return _dot(xn, w_ref[:, a:b])

    qkv_ref[...] = mm(0, P3)
    hq_ref[...] = mm(P3, P4)
    hf_ref[...] = mm(P4, P5)
    hi_ref[...] = mm(P5, P6)
    hg_ref[...] = mm(P6, P7)
    u_ref[...] = mm(P7, P8)
    for n in range(N_BRANCH):
        zg_ref[:, n * D_MODEL:(n + 1) * D_MODEL] = mm(P8 + n * D_MODEL, P8 + (n + 1) * D_MODEL)


def _proj_in(x, norm_w, w_in_bf, layer, tm):
    m = x.shape[0]
    widths = (QKV_W, HG_K, HG_K, HG_K, HG_K, POOL_WIDTH, N_BRANCH * D_MODEL)
    return pl.pallas_call(
        _proj_in_kernel,
        grid=(m // tm,),
        in_specs=[
            pl.BlockSpec((tm, D_MODEL), lambda i: (i, 0)),
            pl.BlockSpec((None, 1, D_MODEL), lambda i: (layer, 0, 0)),
            pl.BlockSpec((None, D_MODEL, D_IN), lambda i: (layer, 0, 0), pipeline_mode=pl.Buffered(1)),
        ],
        out_specs=[pl.BlockSpec((tm, w), lambda i: (i, 0)) for w in widths],
        out_shape=[jax.ShapeDtypeStruct((m, w), F32) for w in widths],
        compiler_params=_cparams("arbitrary"),
        name="proj_in",
    )(x, norm_w, w_in_bf)


def _head_norm_rope(x, w_row, cos, sin, bd):
    width = x.shape[1]
    sq = x * x
    hi = sq.astype(BF16)
    lo = (sq - hi.astype(F32)).astype(BF16)
    parts = []
    for c in range(0, width, 256):
        wc = min(256, width - c)
        b = bd[:wc, :wc]
        parts.append(_dot(hi[:, c:c + wc], b) + _dot(lo[:, c:c + wc], b))
    ms = (parts[0] if len(parts) == 1 else jnp.concatenate(parts, axis=1)) * (1.0 / SWA_HEAD_DIM)
    y = x * lax.rsqrt(ms + EPS) * w_row
    reps = width // 128
    cos_t = cos if reps == 1 else jnp.concatenate([cos] * reps, axis=1)
    sin_t = sin if reps == 1 else jnp.concatenate([sin] * reps, axis=1)
    lane = lax.broadcasted_iota(jnp.int32, y.shape, 1)
    first_half = (lane & (SWA_HEAD_DIM - 1)) < (SWA_HEAD_DIM // 2)
    half = SWA_HEAD_DIM // 2
    rot = jnp.where(first_half, pltpu.roll(y, width - half, 1), pltpu.roll(y, half, 1))
    return y * cos_t + rot * sin_t


def _kv_pad(x):
    lo = lax.broadcasted_iota(jnp.int32, x.shape, 1) < SWA_HEAD_DIM
    xr = pltpu.roll(x, SWA_HEAD_DIM, 1)
    return (jnp.where(lo, x, 0.0), jnp.where(lo, 0.0, xr), jnp.where(lo, xr, 0.0), jnp.where(lo, 0.0, x))


def _sink_softmax_rows(sm, sink):
    mx = jnp.maximum(jnp.max(sm, axis=-1, keepdims=True), sink)
    e = jnp.exp(sm - mx)
    den = jnp.sum(e, axis=-1, keepdims=True) + jnp.exp(sink - mx)
    return e * (1.0 / den)


def _swa_sample_kernel(sinks_ref, qkv_ref, cos_ref, sin_ref, qn_ref, kn_ref, bd_ref, bias_ref, kb_ref, vb_ref,
                       kstack_ref, vstack_ref, o_ref, kc_out_ref, vc_out_ref, *, layer, t_new):
    del kstack_ref, vstack_ref
    z = qkv_ref[...]
    cos = cos_ref[...]
    sin = sin_ref[...]
    bd = bd_ref[...]
    q = _head_norm_rope(z[:, :SWA_Q], qn_ref[...], cos, sin, bd) * (SWA_HEAD_DIM ** -0.5)
    k = _head_norm_rope(z[:, SWA_Q:SWA_Q + SWA_KV], kn_ref[...], cos, sin, bd)
    v = z[:, SWA_Q + SWA_KV:]

    rows = z.shape[0]
    ng = rows // 8
    nkeys = bias_ref.shape[1]
    nseq, wb = kb_ref.shape[0], kb_ref.shape[1]
    for cache_ref, out_ref, new in ((kb_ref, kc_out_ref, k), (vb_ref, vc_out_ref, v)):
        flat = cache_ref[...].reshape(nseq * wb, SWA_KV)
        out_ref[...] = pltpu.roll(flat, nseq * wb - t_new, 0).reshape(nseq, wb, SWA_KV)
        for b in range(nseq):
            out_ref[b, wb - t_new:wb, :] = new[b * t_new:(b + 1) * t_new]
    lo = lax.broadcasted_iota(jnp.int32, (rows, 128), 1) < SWA_HEAD_DIM
    qw = []
    for h in range(SWA_HEADS):
        qc = q[:, (h // 2) * 128:(h // 2 + 1) * 128]
        want_hi = h // SWA_GROUP == 1
        if (h % 2 == 1) != want_hi:
            qc = pltpu.roll(qc, SWA_HEAD_DIM, 1)
        qw.append((jnp.where(lo, 0.0, qc) if want_hi else jnp.where(lo, qc, 0.0)).reshape(ng, 8, 128))
    qw = jnp.concatenate(qw, axis=1).astype(BF16)

    def keys_of(cache_ref, new):
        cache = cache_ref[...].reshape(ng, (8 // t_new) * wb, SWA_KV)
        pad = jnp.zeros((ng, nkeys - cache.shape[1] - 8, SWA_KV), F32)
        return jnp.concatenate([cache, new.reshape(ng, 8, SWA_KV), pad], axis=1).astype(BF16)

    s = jnp.einsum('gqd,gkd->gqk', qw, keys_of(kb_ref, k), preferred_element_type=F32)
    bias = bias_ref[...]
    p = jnp.concatenate(
        [_sink_softmax_rows(s[:, h * 8:(h + 1) * 8, :] + bias, sinks_ref[layer, h]).astype(BF16)
         for h in range(SWA_HEADS)], axis=1)
    res = jnp.einsum('gqk,gkd->gqd', p, keys_of(vb_ref, v), preferred_element_type=F32)
    res = res.reshape(ng * SWA_HEADS * 8, 128)
    res_r = pltpu.roll(res, SWA_HEAD_DIM, 1)
    res = res.reshape(ng, SWA_HEADS * 8, 128)
    res_r = res_r.reshape(ng, SWA_HEADS * 8, 128)
    lo3 = lo.reshape(ng, 8, 128)
    cols = []
    for c in range(SWA_HEADS // 2):
        kv_hi = (2 * c) // SWA_GROUP == 1
        even = (res_r if kv_hi else res)[:, (2 * c) * 8:(2 * c + 1) * 8, :]
        odd = (res if kv_hi else res_r)[:, (2 * c + 1) * 8:(2 * c + 2) * 8, :]
        cols.append(jnp.where(lo3, even, odd).reshape(rows, 128))
    o_ref[...] = jnp.concatenate(cols, axis=1)


def _swa_sample(qkv, cos, sin, qn, kn, bd, bias, sinks, kbuf, vbuf, stacked_prev, layer, batch, t_new):
    bb = SWA_SAMPLE_BB if batch % SWA_SAMPLE_BB == 0 else batch
    rb = bb * t_new
    wb = kbuf.shape[2]
    row = lambda i: (i, 0)
    const = lambda i: (0, 0)
    cache_blk = pl.BlockSpec((None, bb, wb, SWA_KV), lambda i: (layer, i, 0, 0))
    any_spec = pl.BlockSpec(memory_space=pl.ANY)
    return pl.pallas_call(
        functools.partial(_swa_sample_kernel, layer=layer, t_new=t_new),
        grid=(batch // bb,),
        in_specs=[
            pl.BlockSpec(memory_space=pltpu.SMEM),
            pl.BlockSpec((rb, QKV_W), row),
            pl.BlockSpec((rb, 128), const),
            pl.BlockSpec((rb, 128), const),
            pl.BlockSpec((1, SWA_Q), const),
            pl.BlockSpec((1, SWA_KV), const),
            pl.BlockSpec((256, 256), const),
            pl.BlockSpec(bias.shape, const),
            cache_blk,
            cache_blk,
            any_spec,
            any_spec,
        ],
        out_specs=[pl.BlockSpec((rb, SWA_Q), row), cache_blk, cache_blk],
        out_shape=[
            jax.ShapeDtypeStruct((batch * t_new, SWA_Q), F32),
            jax.ShapeDtypeStruct(kbuf.shape, F32),
            jax.ShapeDtypeStruct(vbuf.shape, F32),
        ],
        input_output_aliases={10: 1, 11: 2},
        compiler_params=_cparams("arbitrary"),
        name="swa_sample",
    )(sinks, qkv, cos, sin, qn, kn, bd, bias, kbuf, vbuf, stacked_prev[0], stacked_prev[1])


def _hgrn_gates(hq, hf, lb):
    sig_p = _sigmoid(hf)
    om = 1.0 - lb
    log2f = jnp.log(jnp.maximum(lb, LB_FLOOR) + om * sig_p) * LOG2E
    return _silu(hq), log2f, om * (1.0 - sig_p)


def _hgrn_out(o, onorm, hg):
    outs = []
    for h in range(HG_HEADS):
        hs = slice(h * HG_DV, (h + 1) * HG_DV)
        outs.append(_rms_rows(o[h], onorm) * _silu(hg[:, hs]))
    return jnp.concatenate(outs, axis=1)


def _col_bcast(row):
    return jnp.transpose(jnp.broadcast_to(row, (row.shape[1], row.shape[1])))


def _hgrn_sample_kernel(hq_ref, hf_ref, hi_ref, hg_ref, lb_ref, on_ref, s0_ref, stack_ref, o_ref, s_out_ref, *,
                        t_new):
    del stack_ref
    q, g, kg = _hgrn_gates(hq_ref[...], hf_ref[...], lb_ref[...])
    v = hi_ref[...]
    rows = q.shape[0]
    t_idx = lax.broadcasted_iota(jnp.int32, (rows, 1), 0) % t_new

    def down(x, d):
        return pltpu.roll(x, d, 0)

    def up(x, d):
        return pltpu.roll(x, rows - d, 0)

    G = g
    for d in range(1, t_new):
        G = G + jnp.where(t_idx >= d, down(g, d), 0.0)
    rest = jnp.zeros_like(g)
    for d in range(1, t_new):
        rest = rest + jnp.where(t_idx + d < t_new, up(g, d), 0.0)
    q_in = q * jnp.exp2(G)
    k_out = kg * jnp.exp2(rest)
    dec_all = jnp.exp2(G + rest)

    o_intra = [jnp.zeros((rows, HG_DV), F32) for _ in range(HG_HEADS)]
    for d in range(t_new):
        kd = kg if d == 0 else down(kg, d)
        vd = v if d == 0 else down(v, d)
        gd = G if d == 0 else down(G, d)
        w = q * kd * jnp.exp2(jnp.minimum(G - gd, 0.0))
        for h in range(HG_HEADS):
            hs = slice(h * HG_DK, (h + 1) * HG_DK)
            a = jnp.sum(w[:, hs], axis=-1, keepdims=True)
            o_intra[h] = o_intra[h] + jnp.where(t_idx >= d, a, 0.0) * vd[:, hs]

    dec_hi = dec_all.astype(BF16).astype(F32)
    dec_r = dec_all - dec_hi
    dec_mid = dec_r.astype(BF16).astype(F32)
    dec_lo = (dec_r - dec_mid).astype(BF16).astype(F32)
    dec_parts = jnp.where(t_idx == 0, dec_hi, jnp.where(t_idx == 1, dec_mid, jnp.where(t_idx == 2, dec_lo, 0.0)))
    seq_of_row = lax.broadcasted_iota(jnp.int32, (rows, 1), 0) // t_new
    ones = jnp.ones((rows, HG_DV), BF16)
    v_bf = v.astype(BF16)
    o = []
    for h in range(HG_HEADS):
        hs = slice(h * HG_DK, (h + 1) * HG_DK)
        o_h = o_intra[h]
        for b in range(s0_ref.shape[0]):
            mine = seq_of_row == b
            s_h = s0_ref[b, h]
            o_h = o_h + _dot(jnp.where(mine, q_in[:, hs], 0.0).astype(BF16), s_h.astype(BF16))
            dec_mat = _dot_tn(jnp.where(mine, dec_parts[:, hs], 0.0).astype(BF16), ones)
            upd = _dot_tn(jnp.where(mine, k_out[:, hs], 0.0).astype(BF16), v_bf[:, hs])
            s_out_ref[b, h] = dec_mat * s_h + upd
        o.append(o_h)
    o_ref[...] = _hgrn_out(o, on_ref[...], hg_ref[...])


def _hgrn_sample(hq, hf, hi, hg, lower, onorm, state, stacked_prev, layer, batch, t_new):
    assert t_new >= 3
    rb = SAMPLE_BB * t_new
    blk = pl.BlockSpec((rb, HG_K), lambda i: (i, 0))
    st_blk = pl.BlockSpec((None, SAMPLE_BB, HG_HEADS, HG_DK, HG_DV), lambda i: (layer, i, 0, 0, 0))
    return pl.pallas_call(
        functools.partial(_hgrn_sample_kernel, t_new=t_new),
        grid=(batch // SAMPLE_BB,),
        in_specs=[
            blk, blk, blk, blk,
            pl.BlockSpec((None, 1, HG_K), lambda i: (layer, 0, 0)),
            pl.BlockSpec((None, 1, HG_DV), lambda i: (layer, 0, 0)),
            st_blk,
            pl.BlockSpec(memory_space=pl.ANY),
        ],
        out_specs=[blk, st_blk],
        out_shape=[
            jax.ShapeDtypeStruct((batch * t_new, HG_K), F32),
            jax.ShapeDtypeStruct(state.shape, state.dtype),
        ],
        input_output_aliases={7: 1},
        compiler_params=_cparams("arbitrary"),
        name="hgrn_sample",
    )(hq, hf, hi, hg, lower, onorm, state, stacked_prev)


def _pool_project(d_groups, pw_ref, ps_ref):
    outs = []
    for gi, d in enumerate(d_groups):
        cs = slice(gi * POOL_GROUP_DIM, (gi + 1) * POOL_GROUP_DIM)
        outs.append(_dot(d.astype(BF16), pw_ref[gi]) * ps_ref[:, cs])
    return jnp.concatenate(outs, axis=1)


def _pool_sample_kernel(pb_ref, u_ref, pw_ref, ps_ref, o_ref, *, t_new):
    def row(i):
        return pb_ref[i] if i < POOL_BUF else u_ref[i - POOL_BUF]

    outs = []
    for t in range(t_new):
        e = POOL_BUF + t
        d_groups = []
        for gi, w in enumerate(POOL_WINDOWS):
            cs = slice(gi * POOL_GROUP_DIM, (gi + 1) * POOL_GROUP_DIM)
            lo = max(e + 1 - w, 0)
            acc = row(lo)[:, cs]
            for i in range(lo + 1, e + 1):
                acc = acc + row(i)[:, cs]
            d_groups.append(acc / float(e + 1 - lo) - u_ref[t][:, cs])
        outs.append(_pool_project(d_groups, pw_ref, ps_ref))
    for t in range(t_new):
        o_ref[t] = outs[t]


def _pool_sample(pbuf_t, u_t, pool_w_bf, pool_scale, layer, t_new):
    batch = u_t.shape[1]
    return pl.pallas_call(
        functools.partial(_pool_sample_kernel, t_new=t_new),
        grid=(1,),
        in_specs=[
            pl.BlockSpec((POOL_BUF, batch, POOL_WIDTH), lambda i: (0, 0, 0)),
            pl.BlockSpec((t_new, batch, POOL_WIDTH), lambda i: (0, 0, 0)),
            pl.BlockSpec((None, len(POOL_WINDOWS), POOL_GROUP_DIM, POOL_GROUP_DIM), lambda i: (layer, 0, 0, 0)),
            pl.BlockSpec((None, 1, POOL_WIDTH), lambda i: (layer, 0, 0)),
        ],
        out_specs=pl.BlockSpec((t_new, batch, POOL_WIDTH), lambda i: (0, 0, 0)),
        out_shape=jax.ShapeDtypeStruct((t_new, batch, POOL_WIDTH), F32),
        compiler_params=_cparams("arbitrary"),
        name="pool_sample",
    )(pbuf_t, u_t, pool_w_bf, pool_scale)


def _merge_kernel(x_ref, oa_ref, ob_ref, oc_ref, zg_ref, wb_ref, wo_ref, o_ref):
    m = None
    for n, br in enumerate((oa_ref, ob_ref, oc_ref)):
        proj = _dot(br[...].astype(BF16), wb_ref[n])
        gate = _sigmoid(zg_ref[:, n * D_MODEL:(n + 1) * D_MODEL])
        m = gate * proj if m is None else m + gate * proj
    o_ref[...] = x_ref[...] + _dot(m.astype(BF16), wo_ref[...])


def _merge(x, oa, ob, oc, zg, w_branch_bf, w_o_bf, layer, tm):
    m = x.shape[0]
    row = lambda i: (i, 0)
    return pl.pallas_call(
        _merge_kernel,
        grid=(m // tm,),
        in_specs=[
            pl.BlockSpec((tm, D_MODEL), row),
            pl.BlockSpec((tm, BRANCH_WIDTH), row),
            pl.BlockSpec((tm, BRANCH_WIDTH), row),
            pl.BlockSpec((tm, BRANCH_WIDTH), row),
            pl.BlockSpec((tm, N_BRANCH * D_MODEL), row),
            pl.BlockSpec((None, N_BRANCH, BRANCH_WIDTH, D_MODEL), lambda i: (layer, 0, 0, 0)),
            pl.BlockSpec((None, D_MODEL, D_MODEL), lambda i: (layer, 0, 0)),
        ],
        out_specs=pl.BlockSpec((tm, D_MODEL), row),
        out_shape=jax.ShapeDtypeStruct((m, D_MODEL), F32),
        compiler_params=_cparams("arbitrary"),
        name="merge",
    )(x, oa, ob, oc, zg, w_branch_bf, w_o_bf)


_Z_GROUPS = ((0, P3), (P3, P7), (P7, P8)) + tuple(
    (P8 + n * D_MODEL, P8 + (n + 1) * D_MODEL) for n in range(N_BRANCH))


def _project_tile(x, nw, w_ref, z_ref):
    xn = _rms_rows(x, nw).astype(BF16)
    for a, b in _Z_GROUPS:
        z_ref[:, a:b] = _dot(xn, w_ref[:, a:b])


def _swa_blocks(z, cos, sin, qn, kn, bd, biases, sinks_ref, layer, kprep, vprep):
    nblk = len(biases)
    q = _head_norm_rope(z[:, :SWA_Q], qn, cos, sin, bd) * (SWA_HEAD_DIM ** -0.5)
    k = _head_norm_rope(z[:, SWA_Q:SWA_Q + SWA_KV], kn, cos, sin, bd)
    v = z[:, SWA_Q + SWA_KV:]
    kc = [a.astype(BF16) for a in _kv_pad(k)]
    vc = [a.astype(BF16) for a in _kv_pad(v)]

    def blk(x, j):
        return x[j * WINDOW:(j + 1) * WINDOW]

    def prev_cur(prep, cur, pad, j):
        return (prep[pad] if j == 0 else blk(cur[pad], j - 1)), blk(cur[pad], j)

    bias_all = jnp.concatenate(biases, axis=0)
    out_cols = [[None] * nblk for _ in range(SWA_HEADS // 2)]
    for kvh in range(SWA_KV_HEADS):
        a, b = 2 * kvh, 2 * kvh + 1
        scores = []
        for j in range(nblk):
            keys = jnp.concatenate(prev_cur(kprep, kc, a, j) + prev_cur(kprep, kc, b, j), axis=0)
            qq = jnp.concatenate([blk(q[:, a * 128:(a + 1) * 128], j), blk(q[:, b * 128:(b + 1) * 128], j)], axis=0)
            scores.append(_dot_nt(qq.astype(BF16), keys))
        probs = [[[None, None] for _ in range(2)] for _ in range(nblk)]
        for half in range(2):
            for r in range(2):
                sm = jnp.concatenate(
                    [s[r * WINDOW:(r + 1) * WINDOW, half * 2 * WINDOW:(half + 1) * 2 * WINDOW] for s in scores],
                    axis=0) + bias_all
                p = _sink_softmax_rows(sm, sinks_ref[layer, SWA_GROUP * kvh + 2 * r + half]).astype(BF16)
                for j in range(nblk):
                    probs[j][half][r] = blk(p, j)
        for j in range(nblk):
            vals = jnp.concatenate(prev_cur(vprep, vc, a, j) + prev_cur(vprep, vc, b, j), axis=0)
            p = jnp.concatenate([jnp.concatenate(probs[j][half], axis=0) for half in range(2)], axis=1)
            og = _dot(p, vals)
            out_cols[a][j] = og[:WINDOW]
            out_cols[b][j] = og[WINDOW:]
    for a in range(2 * SWA_KV_HEADS):
        kprep[a] = blk(kc[a], nblk - 1)
        vprep[a] = blk(vc[a], nblk - 1)
    o = jnp.concatenate([jnp.concatenate(col, axis=0) for col in out_cols], axis=1)
    return o, blk(k, nblk - 1), blk(v, nblk - 1)


def _hgrn_chunk(hq, hf, v, hg, lb, onorm, tri, lvl, s_scr):
    c = HG_CHUNK
    q, g, kg = _hgrn_gates(hq, hf, lb)
    g_hi, g_mid, g_lo = _split3(g)
    G = _dot(tri, g_hi) + _dot(tri, g_mid) + _dot(tri, g_lo)
    t_col = lax.broadcasted_iota(jnp.int32, (c, 1), 0)
    levels = []
    half = HG_DIAG
    while half < c:
        blk = 2 * half
        gb = jnp.concatenate(
            [jnp.broadcast_to(G[p * blk + half - 1:p * blk + half, :], (blk, G.shape[1]))
             for p in range(c // blk)], axis=0)
        second = (t_col & half) != 0
        e = jnp.exp2(jnp.where(second, G - gb, gb - G))
        levels.append(((q * e).astype(BF16), (kg * e).astype(BF16), lvl == len(levels)))
        half = blk
    g_last = G[c - 1:c, :]
    q_in = (q * jnp.exp2(G)).astype(BF16)
    k_out = (kg * jnp.exp2(jnp.minimum(g_last - G, 0.0))).astype(BF16)
    dec_last = jnp.exp2(g_last)
    v_bf = v.astype(BF16)
    nblk = c // HG_DIAG
    row_in_blk = lax.broadcasted_iota(jnp.int32, (nblk, HG_DIAG, HG_DV), 1)
    o_heads = []
    for h in range(HG_HEADS):
        hs = slice(h * HG_DK, (h + 1) * HG_DK)
        s_h = s_scr[h]
        a_off = jnp.zeros((c, c), F32)
        for qt, kt, same in levels:
            a_off = a_off + jnp.where(same, _dot_nt(qt[:, hs], kt[:, hs]), 0.0)
        o_h = _dot(q_in[:, hs], s_h.astype(BF16)) + _dot(a_off.astype(BF16), v_bf[:, hs])
        q3 = q[:, hs].reshape(nblk, HG_DIAG, HG_DK)
        k3 = kg[:, hs].reshape(nblk, HG_DIAG, HG_DK)
        g3 = G[:, hs].reshape(nblk, HG_DIAG, HG_DK)
        v3 = v[:, hs].reshape(nblk, HG_DIAG, HG_DV)
        od = jnp.zeros((nblk, HG_DIAG, HG_DV), F32)
        for s in range(HG_DIAG):
            e = jnp.exp2(jnp.minimum(g3 - g3[:, s:s + 1, :], 0.0))
            a = jnp.sum(q3 * (k3[:, s:s + 1, :] * e), axis=-1, keepdims=True)
            od = od + a * jnp.where(row_in_blk >= s, v3[:, s:s + 1, :], 0.0)
        o_heads.append(o_h + od.reshape(c, HG_DV))
        s_scr[h] = _col_bcast(dec_last[:, hs]) * s_h + _dot_tn(k_out[:, hs], v_bf[:, hs])
    return _hgrn_out(o_heads, onorm, hg)


def _pool_tile(u, carry, pos0, pw_ref, ps_ref):
    tt = u.shape[0]
    x = jnp.concatenate([carry[...], u], axis=0)
    sums = []
    cur = x
    shift = 1
    for gi in range(len(POOL_WINDOWS)):
        cur = cur + pltpu.roll(cur, shift, 0)
        sums.append(cur[POOL_CARRY:, :POOL_GROUP_DIM])
        cur = cur[:, POOL_GROUP_DIM:]
        shift *= 2
    pos = pos0 + lax.broadcasted_iota(jnp.int32, (tt, 1), 0)
    d_groups = []
    for gi, w in enumerate(POOL_WINDOWS):
        cs = slice(gi * POOL_GROUP_DIM, (gi + 1) * POOL_GROUP_DIM)
        cnt = jnp.minimum(w, pos + 1).astype(F32)
        d_groups.append(sums[gi] / cnt - u[:, cs])
    carry[...] = u[tt - POOL_CARRY:, :]
    return _pool_project(d_groups, pw_ref, ps_ref)


def _layer_mix_kernel(sinks_ref, x_ref, nw_ref, w_ref, cos_ref, sin_ref, qn_ref, kn_ref, bd_ref,
                      bias_ref, lb_ref, on_ref, tri_ref, lvl_ref, pw_ref, ps_ref, wb_ref, wo_ref,
                      x1_ref, kc_ref, vc_ref, s_out_ref, ulast_ref,
                      z_ref, kprep, vprep, s_scr, carry, *, layer, steps_per_seq):
    n = pl.program_id(0) % steps_per_seq

    @pl.when(n == 0)
    def _():
        kprep[...] = jnp.zeros_like(kprep)
        vprep[...] = jnp.zeros_like(vprep)
        s_scr[...] = jnp.zeros_like(s_scr)
        carry[...] = jnp.zeros_like(carry)

    qn = qn_ref[...]
    kn = kn_ref[...]
    bd = bd_ref[...]
    lb = lb_ref[...]
    onorm = on_ref[...]
    tri = tri_ref[...]
    lvl = lvl_ref[...]
    band = bias_ref[1]
    first = bias_ref[jnp.minimum(n, 1)]

    _project_tile(x_ref[...], nw_ref[...], w_ref, z_ref)
    nblk = MIX_ROWS // WINDOW
    o_b, k, v = _swa_blocks(z_ref[:, 0:P3], cos_ref[...], sin_ref[...], qn, kn, bd,
                            [first] + [band] * (nblk - 1), sinks_ref, layer, kprep, vprep)
    kc_ref[...] = k
    vc_ref[...] = v
    o_a = []
    for j in range(nblk):
        rz = slice(j * WINDOW, (j + 1) * WINDOW)
        o_a.append(_hgrn_chunk(z_ref[rz, P3:P4], z_ref[rz, P4:P5], z_ref[rz, P5:P6], z_ref[rz, P6:P7],
                               lb, onorm, tri, lvl, s_scr))
    for h in range(HG_HEADS):
        s_out_ref[h] = s_scr[h]
    u = z_ref[:, P7:P8]
    ulast_ref[...] = u[MIX_ROWS - POOL_CARRY:, :]
    o_c = _pool_tile(u, carry, n * MIX_ROWS, pw_ref, ps_ref)
    m = None
    for nb, br in enumerate((jnp.concatenate(o_a, axis=0), o_b, o_c)):
        gate = _sigmoid(z_ref[:, P8 + nb * D_MODEL:P8 + (nb + 1) * D_MODEL])
        term = gate * _dot(br.astype(BF16), wb_ref[nb])
        m = term if m is None else m + term
    x1_ref[...] = x_ref[...] + _dot(m.astype(BF16), wo_ref[...])


def _layer_mix(x, norm_w, w_in_bf, cos, sin, qn, kn, bd, bias, sinks, lower, onorm, tri, lvl, pool_w_bf, pool_scale,
               w_branch_bf, w_o_bf, layer, batch, seq):
    step_rows = MIX_ROWS
    sps = seq // step_rows
    nsteps = batch * sps
    const2 = lambda s: (0, 0)
    lay3 = lambda s: (layer, 0, 0)
    per_seq = lambda s: (s // sps, 0, 0)
    single = pl.Buffered(1)
    return pl.pallas_call(
        functools.partial(_layer_mix_kernel, layer=layer, steps_per_seq=sps),
        grid=(nsteps,),
        in_specs=[
            pl.BlockSpec(memory_space=pltpu.SMEM),
            pl.BlockSpec((step_rows, D_MODEL), lambda s: (s, 0)),
            pl.BlockSpec((None, 1, D_MODEL), lay3),
            pl.BlockSpec((None, D_MODEL, D_IN), lay3, pipeline_mode=single),
            pl.BlockSpec((step_rows, 128), lambda s: (s % sps, 0)),
            pl.BlockSpec((step_rows, 128), lambda s: (s % sps, 0)),
            pl.BlockSpec((1, SWA_Q), const2),
            pl.BlockSpec((1, SWA_KV), const2),
            pl.BlockSpec((256, 256), const2),
            pl.BlockSpec((2, WINDOW, 2 * WINDOW), lambda s: (0, 0, 0)),
            pl.BlockSpec((None, 1, HG_K), lay3),
            pl.BlockSpec((None, 1, HG_DV), lay3),
            pl.BlockSpec((HG_CHUNK, HG_CHUNK), const2),
            pl.BlockSpec((HG_CHUNK, HG_CHUNK), const2),
            pl.BlockSpec((None, len(POOL_WINDOWS), POOL_GROUP_DIM, POOL_GROUP_DIM), lambda s: (layer, 0, 0, 0)),
            pl.BlockSpec((None, 1, POOL_WIDTH), lay3),
            pl.BlockSpec((None, N_BRANCH, BRANCH_WIDTH, D_MODEL), lambda s: (layer, 0, 0, 0), pipeline_mode=single),
            pl.BlockSpec((None, D_MODEL, D_MODEL), lay3, pipeline_mode=single),
        ],
        out_specs=[
            pl.BlockSpec((step_rows, D_MODEL), lambda s: (s, 0)),
            pl.BlockSpec((None, WINDOW, SWA_KV), per_seq),
            pl.BlockSpec((None, WINDOW, SWA_KV), per_seq),
            pl.BlockSpec((None, HG_HEADS, HG_DK, HG_DV), lambda s: (s // sps, 0, 0, 0)),
            pl.BlockSpec((None, POOL_CARRY, POOL_WIDTH), per_seq),
        ],
        out_shape=[
            jax.ShapeDtypeStruct((batch * seq, D_MODEL), F32),
            jax.ShapeDtypeStruct((batch, WINDOW, SWA_KV), F32),
            jax.ShapeDtypeStruct((batch, WINDOW, SWA_KV), F32),
            jax.ShapeDtypeStruct((batch, HG_HEADS, HG_DK, HG_DV), F32),
            jax.ShapeDtypeStruct((batch, POOL_CARRY, POOL_WIDTH), F32),
        ],
        scratch_shapes=[
            pltpu.VMEM((MIX_ROWS, D_IN), F32),
            pltpu.VMEM((2 * SWA_KV_HEADS, WINDOW, 128), BF16),
            pltpu.VMEM((2 * SWA_KV_HEADS, WINDOW, 128), BF16),
            pltpu.VMEM((HG_HEADS, HG_DK, HG_DV), F32),
            pltpu.VMEM((POOL_CARRY, POOL_WIDTH), F32),
        ],
        compiler_params=_cparams("arbitrary"),
        name="layer_mix",
    )(sinks, x, norm_w, w_in_bf, cos, sin, qn, kn, bd, bias, lower, onorm, tri, lvl, pool_w_bf, pool_scale,
      w_branch_bf, w_o_bf)


def _ffn_kernel(x_ref, nw_ref, wu_ref, wd_ref, o_ref):
    x = x_ref[...]
    xn = _rms_rows(x, nw_ref[...]).astype(BF16)
    acc = x
    step = 2 * D_MODEL
    for c in range(0, D_FF, step):
        h = _dot(xn, wu_ref[:, c:c + step])
        h = jnp.square(jnp.maximum(h, 0.0)).astype(BF16)
        acc = acc + _dot(h, wd_ref[c:c + step, :])
    o_ref[...] = acc


def _ffn(x, norm_w, w_up_bf, w_down_bf, layer, tm):
    m = x.shape[0]
    row = lambda i: (i, 0)
    return pl.pallas_call(
        _ffn_kernel,
        grid=(m // tm,),
        in_specs=[
            pl.BlockSpec((tm, D_MODEL), row),
            pl.BlockSpec((None, 1, D_MODEL), lambda i: (layer, 0, 0)),
            pl.BlockSpec((None, D_MODEL, D_FF), lambda i: (layer, 0, 0), pipeline_mode=pl.Buffered(1)),
            pl.BlockSpec((None, D_FF, D_MODEL), lambda i: (layer, 0, 0), pipeline_mode=pl.Buffered(1)),
        ],
        out_specs=pl.BlockSpec((tm, D_MODEL), row),
        out_shape=jax.ShapeDtypeStruct((m, D_MODEL), F32),
        compiler_params=_cparams("arbitrary"),
        name="ffn",
    )(x, norm_w, w_up_bf, w_down_bf)


def _rope_tables(pos):
    hd = SWA_HEAD_DIM
    inv = jnp.power(ROPE_THETA, -jnp.arange(0, hd, 2, dtype=F32) / hd)
    ang = pos[:, None] * inv[None, :]
    cos = jnp.cos(ang)
    sin = jnp.sin(ang)
    cos_t = jnp.concatenate([cos, cos, cos, cos], axis=1)
    sin_t = jnp.concatenate([-sin, sin, -sin, sin], axis=1)
    return cos_t, sin_t


def _hgrn_levels():
    t = jnp.arange(HG_CHUNK)[:, None]
    s = jnp.arange(HG_CHUNK)[None, :]
    lvl = jnp.full((HG_CHUNK, HG_CHUNK), -1, jnp.int32)
    half, level = HG_DIAG, 0
    while half < HG_CHUNK:
        shift = (2 * half).bit_length() - 1
        own = ((t >> shift) == (s >> shift)) & ((t & half) != 0) & ((s & half) == 0)
        lvl = jnp.where(own, level, lvl)
        half, level = 2 * half, level + 1
    return lvl


def _prompt_bias():
    i = jnp.arange(WINDOW)[:, None]
    j = jnp.arange(2 * WINDOW)[None, :]
    band = (j > i) & (j <= i + WINDOW)
    first = band & (j >= WINDOW)
    return jnp.where(jnp.stack([first, band]), 0.0, NEG).astype(F32)


def _sample_bias(wb, t_new):
    grp = 8 // t_new
    n_cache = grp * wb
    nkeys = -(-(n_cache + 8) // 128) * 128
    r = jnp.arange(8)[:, None]
    seq_r, t = r // t_new, r % t_new
    j = jnp.arange(nkeys)[None, :]
    cache_ok = (j < n_cache) & (j // wb == seq_r) & (wb + t - j % wb < WINDOW)
    jn = j - n_cache
    new_ok = (jn >= 0) & (jn < 8) & (jn // t_new == seq_r) & (jn % t_new <= t)
    return jnp.where(cache_ok | new_ok, 0.0, NEG).astype(F32)


def _row_tile(m, cap):
    t = min(m, cap)
    while m % t:
        t //= 2
    return t


def kernel(x_prompt, x_sample, state_hgrn, cache_swa_k, cache_swa_v, state_pool, norm_mix, w_in, q_norm, k_norm,
           attn_sinks, hgrn_lb, hgrn_onorm, pool_w, pool_scale, w_branch, w_o, norm_ffn, w_up, w_down):
    depth = w_in.shape[0]
    bp, seq, _ = x_prompt.shape
    bs, t_new, _ = x_sample.shape
    wb = cache_swa_k.shape[2]

    w_in_bf = w_in.astype(BF16)
    w_branch_bf = w_branch.astype(BF16)
    w_o_bf = w_o.astype(BF16)
    w_up_bf = w_up.astype(BF16)
    w_down_bf = w_down.astype(BF16)
    pool_w_bf = pool_w.astype(BF16)
    norm_mix3 = norm_mix.reshape(depth, 1, D_MODEL)
    norm_ffn3 = norm_ffn.reshape(depth, 1, D_MODEL)
    onorm3 = hgrn_onorm.reshape(depth, 1, HG_DV)
    pool_scale3 = pool_scale.reshape(depth, 1, POOL_WIDTH)
    lower3 = _lower_bounds(hgrn_lb).reshape(depth, 1, HG_K)
    cache_k4 = cache_swa_k.reshape(depth, bs, wb, SWA_KV)
    cache_v4 = cache_swa_v.reshape(depth, bs, wb, SWA_KV)

    cos_p, sin_p = _rope_tables(jnp.arange(seq, dtype=F32))
    cos_s, sin_s = _rope_tables(jnp.arange(t_new, dtype=F32) + PAST_LEN)
    bb_swa = SWA_SAMPLE_BB if bs % SWA_SAMPLE_BB == 0 else bs
    cos_s = jnp.tile(cos_s, (bb_swa, 1))
    sin_s = jnp.tile(sin_s, (bb_swa, 1))
    bias_p = _prompt_bias()
    bias_s = _sample_bias(wb, t_new)
    lane = jnp.arange(256)
    bd = (lane[:, None] // SWA_HEAD_DIM == lane[None, :] // SWA_HEAD_DIM).astype(BF16)
    tok = jnp.arange(HG_CHUNK)
    tri = (tok[:, None] >= tok[None, :]).astype(BF16)
    lvl = _hgrn_levels()

    hp = x_prompt.reshape(bp * seq, D_MODEL)
    hs = x_sample.reshape(bs * t_new, D_MODEL)
    tm_p = _row_tile(bp * seq, 512)
    tm_s = _row_tile(bs * t_new, 512)

    sp = ([], [], [], [])
    ss = ([], [], [], [])
    kv_stacks = (cache_k4, cache_v4)
    s_stack = jnp.zeros(state_hgrn.shape, state_hgrn.dtype)
    for l in range(depth):
        qn = jnp.tile(q_norm[l], SWA_HEADS).reshape(1, SWA_Q)
        kn = jnp.tile(k_norm[l], SWA_KV_HEADS).reshape(1, SWA_KV)

        x1, k_last, v_last, s_new, u_last = _layer_mix(
            hp, norm_mix3, w_in_bf, cos_p, sin_p, qn, kn, bd, bias_p, attn_sinks, lower3, onorm3, tri, lvl,
            pool_w_bf, pool_scale3, w_branch_bf, w_o_bf, l, bp, seq)
        hp = _ffn(x1, norm_ffn3, w_up_bf, w_down_bf, l, tm_p)
        sp[0].append(s_new)
        sp[1].append(k_last.reshape(bp, WINDOW, SWA_KV_HEADS, SWA_HEAD_DIM))
        sp[2].append(v_last.reshape(bp, WINDOW, SWA_KV_HEADS, SWA_HEAD_DIM))
        sp[3].append(u_last[:, POOL_CARRY - POOL_BUF:])

        qkv, hq, hf, hi, hg, u, zg = _proj_in(hs, norm_mix3, w_in_bf, l, tm_s)
        o_b, k_stack, v_stack = _swa_sample(qkv, cos_s, sin_s, qn, kn, bd, bias_s, attn_sinks, cache_k4, cache_v4,
                                            kv_stacks, l, bs, t_new)
        kv_stacks = (k_stack, v_stack)
        o_a, s_stack = _hgrn_sample(hq, hf, hi, hg, lower3, onorm3, state_hgrn, s_stack, l, bs, t_new)
        u3 = u.reshape(bs, t_new, POOL_WIDTH)
        o_c = _pool_sample(jnp.swapaxes(state_pool[l], 0, 1), jnp.swapaxes(u3, 0, 1),
                           pool_w_bf, pool_scale3, l, t_new)
        o_c = jnp.swapaxes(o_c, 0, 1).reshape(bs * t_new, POOL_WIDTH)
        x1 = _merge(hs, o_a, o_b, o_c, zg, w_branch_bf, w_o_bf, l, tm_s)
        hs = _ffn(x1, norm_ffn3, w_up_bf, w_down_bf, l, tm_s)
        ss[3].append(jnp.concatenate([state_pool[l], u3], axis=1)[:, -POOL_BUF:])

    cache_shape = (depth, bs, wb, SWA_KV_HEADS, SWA_HEAD_DIM)
    return (hp.reshape(bp, seq, D_MODEL), hs.reshape(bs, t_new, D_MODEL),
            jnp.stack(sp[0]), jnp.stack(sp[1]), jnp.stack(sp[2]), jnp.stack(sp[3]),
            s_stack, kv_stacks[0].reshape(cache_shape), kv_stacks[1].reshape(cache_shape),
            jnp.stack(ss[3]))
```
